```python
import math
import jax
import jax.numpy as jnp
from jax import lax
import numpy as np

D_MODEL = 1024
BATCH = 8
SEQ = 4096
DEPTH = 4

D_MIX = 2 * D_MODEL
D_SSD = D_MODEL
D_GMLP = D_MIX - D_SSD
SSD_HEAD_DIM = 64
SSD_HEADS = D_SSD // SSD_HEAD_DIM
SSD_GROUPS = 4
HEADS_PER_GROUP = SSD_HEADS // SSD_GROUPS
D_STATE = 128
CONV_WIDTH = 5
CONV_DIM = D_SSD + 2 * SSD_GROUPS * D_STATE
CHUNK = 128
GMLP_GROUPS = 8
GMLP_GROUP_DIM = D_GMLP // GMLP_GROUPS
D_FF = ((8 * D_MODEL // 3 + 127) // 128) * 128
FFN_RESIDUAL_WEIGHT = 0.5
D_IN_PROJ = D_SSD + CONV_DIM + 2 * SSD_HEADS + 2 * D_GMLP
SPLIT_IDX = (D_SSD,
             D_SSD + CONV_DIM,
             D_SSD + CONV_DIM + SSD_HEADS,
             D_SSD + CONV_DIM + 2 * SSD_HEADS,
             D_SSD + CONV_DIM + 2 * SSD_HEADS + D_GMLP)
EPS = 1e-6

kernel_name = 'hybrid_ssd_gmlp_macaron_sandwich_encoder'


def rms_norm(x, g):
    xf = x.astype(jnp.float32)
    xf = xf * lax.rsqrt(jnp.mean(xf * xf, axis=-1, keepdims=True) + EPS)
    return (xf * g.astype(jnp.float32)).astype(x.dtype)


def layer_norm(x, g, b):
    xf = x.astype(jnp.float32)
    mu = jnp.mean(xf, axis=-1, keepdims=True)
    xc = xf - mu
    xf = xc * lax.rsqrt(jnp.mean(xc * xc, axis=-1, keepdims=True) + EPS)
    return (xf * g.astype(jnp.float32) + b.astype(jnp.float32)).astype(x.dtype)


def swiglu(h, w_gate, w_up, w_down):
    return (jax.nn.silu(h @ w_gate) * (h @ w_up)) @ w_down


def centred_depthwise_conv(x, w, b):
    pad = (CONV_WIDTH - 1) // 2
    y = lax.conv_general_dilated(
        x, w[:, None, :].astype(x.dtype), window_strides=(1,),
        padding=[(pad, pad)], dimension_numbers=('NWC', 'WIO', 'NWC'),
        feature_group_count=x.shape[-1])
    return y + b


def ssd_chunked(x, dt, a, bm, cm):
    bsz, seqlen = x.shape[:2]
    nc = seqlen // CHUNK
    xc = x.reshape(bsz, nc, CHUNK, SSD_GROUPS, HEADS_PER_GROUP, SSD_HEAD_DIM)
    dtc = dt.reshape(bsz, nc, CHUNK, SSD_GROUPS, HEADS_PER_GROUP)
    bc = bm.reshape(bsz, nc, CHUNK, SSD_GROUPS, D_STATE)
    cc = cm.reshape(bsz, nc, CHUNK, SSD_GROUPS, D_STATE)
    cs = jnp.cumsum(dtc * a, axis=2)
    xdt = xc * dtc[..., None]
    seg = cs[:, :, :, None] - cs[:, :, None, :]
    lower = jnp.tril(jnp.ones((CHUNK, CHUNK), dtype=bool))[:, :, None, None]
    lmat = jnp.exp(jnp.where(lower, seg, -jnp.inf))
    cb = jnp.einsum('bcign,bcjgn->bcijg', cc, bc)
    y_diag = jnp.einsum('bcijgr,bcjgrp->bcigrp', cb[..., None] * lmat, xdt)
    decay_states = jnp.exp(cs[:, :, -1:] - cs)
    states = jnp.einsum('bcjgn,bcjgr,bcjgrp->bcgrpn', bc, decay_states, xdt)
    chunk_decay = jnp.exp(cs[:, :, -1])

    def step(carry, inp):
        s, d = inp
        return carry * d[..., None, None] + s, carry

    init = jnp.zeros_like(states[:, 0])
    _, prev = lax.scan(step, init, (jnp.moveaxis(states, 1, 0), jnp.moveaxis(chunk_decay, 1, 0)))
    prev = jnp.moveaxis(prev, 0, 1)
    y_off = jnp.einsum('bcign,bcgrpn->bcigrp', cc, prev) * jnp.exp(cs)[..., None]
    return (y_diag + y_off).reshape(bsz, seqlen, SSD_GROUPS, HEADS_PER_GROUP, SSD_HEAD_DIM)


def token_mix(h, w_in, conv_w, conv_b, dt_bias_f, dt_bias_b, a_log_f, a_log_b, d_skip,
              ssd_norm_g, gmlp_ln_g, gmlp_ln_b, spatial_w, spatial_b, w_out):
    bsz, seqlen, _ = h.shape
    f32 = jnp.float32
    proj = h @ w_in
    z, xbc, dt_f, dt_b, u, v = jnp.split(proj, SPLIT_IDX, axis=-1)

    xbc = jax.nn.silu(centred_depthwise_conv(xbc, conv_w, conv_b))
    xs, bm, cm = jnp.split(xbc, (D_SSD, D_SSD + SSD_GROUPS * D_STATE), axis=-1)
    xs = xs.astype(f32).reshape(bsz, seqlen, SSD_GROUPS, HEADS_PER_GROUP, SSD_HEAD_DIM)
    bm = bm.astype(f32).reshape(bsz, seqlen, SSD_GROUPS, D_STATE)
    cm = cm.astype(f32).reshape(bsz, seqlen, SSD_GROUPS, D_STATE)
    hshape = (bsz, seqlen, SSD_GROUPS, HEADS_PER_GROUP)
    dtf = jax.nn.softplus(dt_f.astype(f32) + dt_bias_f.astype(f32)).reshape(hshape)
    dtb = jax.nn.softplus(dt_b.astype(f32) + dt_bias_b.astype(f32)).reshape(hshape)
    a_f = -jnp.exp(a_log_f.astype(f32)).reshape(SSD_GROUPS, HEADS_PER_GROUP)
    a_b = -jnp.exp(a_log_b.astype(f32)).reshape(SSD_GROUPS, HEADS_PER_GROUP)
    flip = lambda t: jnp.flip(t, axis=1)
    y_fwd = ssd_chunked(xs, dtf, a_f, bm, cm)
    y_bwd = flip(ssd_chunked(flip(xs), flip(dtb), a_b, flip(bm), flip(cm)))
    d = d_skip.astype(f32).reshape(SSD_GROUPS, HEADS_PER_GROUP)[..., None]
    y = (y_fwd + y_bwd + d * xs).reshape(bsz, seqlen, D_SSD)
    y_ssd = rms_norm(y * jax.nn.silu(z.astype(f32)), ssd_norm_g).astype(h.dtype)

    u = jax.nn.gelu(u, approximate=False)
    v = layer_norm(jax.nn.gelu(v, approximate=False), gmlp_ln_g, gmlp_ln_b)
    nc = seqlen // CHUNK
    vc = v.reshape(bsz, nc, CHUNK, GMLP_GROUPS, GMLP_GROUP_DIM)
    mixed = jnp.einsum('gij,bcjgd->bcigd', spatial_w, vc) + spatial_b.T[None, None, :, :, None]
    y_gmlp = u * mixed.reshape(bsz, seqlen, D_GMLP)

    return jnp.concatenate([y_ssd, y_gmlp], axis=-1) @ w_out


def setup_inputs(seed: int = 0) -> dict:
    key = jax.random.key(seed)
    ks = list(jax.random.split(key, 32))
    nrm = lambda k, shape, scale: scale * jax.random.normal(k, shape, jnp.float32)
    gain = lambda k, shape: 1.0 + 0.02 * jax.random.normal(k, shape, jnp.float32)
    L = DEPTH
    u_dt_f = jax.random.uniform(ks[10], (L, SSD_HEADS), jnp.float32)
    u_dt_b = jax.random.uniform(ks[11], (L, SSD_HEADS), jnp.float32)
    lo, hi = math.log(1e-3), math.log(1e-1)
    dt_f0 = jnp.exp(u_dt_f * (hi - lo) + lo)
    dt_b0 = jnp.exp(u_dt_b * (hi - lo) + lo)
    return {
        'x': jax.random.normal(ks[0], (BATCH, SEQ, D_MODEL), jnp.float32),
        'ff1_pre_g': gain(ks[1], (L, D_MODEL)),
        'ff1_w_gate': nrm(ks[2], (L, D_MODEL, D_FF), D_MODEL ** -0.5),
        'ff1_w_up': nrm(ks[3], (L, D_MODEL, D_FF), D_MODEL ** -0.5),
        'ff1_w_down': nrm(ks[4], (L, D_FF, D_MODEL), D_FF ** -0.5),
        'ff1_post_g': gain(ks[5], (L, D_MODEL)),
        'mix_pre_g': gain(ks[6], (L, D_MODEL)),
        'w_in': nrm(ks[7], (L, D_MODEL, D_IN_PROJ), D_MODEL ** -0.5),
        'conv_w': nrm(ks[8], (L, CONV_WIDTH, CONV_DIM), CONV_WIDTH ** -0.5),
        'conv_b': nrm(ks[9], (L, CONV_DIM), 0.02),
        'dt_bias_f': dt_f0 + jnp.log(-jnp.expm1(-dt_f0)),
        'dt_bias_b': dt_b0 + jnp.log(-jnp.expm1(-dt_b0)),
        'a_log_f': jnp.log(jax.random.uniform(ks[12], (L, SSD_HEADS), jnp.float32, 1.0, 16.0)),
        'a_log_b': jnp.log(jax.random.uniform(ks[13], (L, SSD_HEADS), jnp.float32, 1.0, 16.0)),
        'd_skip': gain(ks[14], (L, SSD_HEADS)),
        'ssd_norm_g': gain(ks[15], (L, D_SSD)),
        'gmlp_ln_g': gain(ks[16], (L, D_GMLP)),
        'gmlp_ln_b': nrm(ks[17], (L, D_GMLP), 0.02),
        'spatial_w': nrm(ks[18], (L, GMLP_GROUPS, CHUNK, CHUNK), CHUNK ** -0.5),
        'spatial_b': gain(ks[19], (L, GMLP_GROUPS, CHUNK)),
        'w_out': nrm(ks[20], (L, D_MIX, D_MODEL), D_MIX ** -0.5),
        'mix_post_g': gain(ks[21], (L, D_MODEL)),
        'ff2_pre_g': gain(ks[22], (L, D_MODEL)),
        'ff2_w_gate': nrm(ks[23], (L, D_MODEL, D_FF), D_MODEL ** -0.5),
        'ff2_w_up': nrm(ks[24], (L, D_MODEL, D_FF), D_MODEL ** -0.5),
        'ff2_w_down': nrm(ks[25], (L, D_FF, D_MODEL), D_FF ** -0.5),
        'ff2_post_g': gain(ks[26], (L, D_MODEL)),
    }


def reference(x, ff1_pre_g, ff1_w_gate, ff1_w_up, ff1_w_down, ff1_post_g,
              mix_pre_g, w_in, conv_w, conv_b, dt_bias_f, dt_bias_b, a_log_f, a_log_b,
              d_skip, ssd_norm_g, gmlp_ln_g, gmlp_ln_b, spatial_w, spatial_b, w_out,
              mix_post_g, ff2_pre_g, ff2_w_gate, ff2_w_up, ff2_w_down, ff2_post_g):
    for l in range(DEPTH):
        f = swiglu(rms_norm(x, ff1_pre_g[l]), ff1_w_gate[l], ff1_w_up[l], ff1_w_down[l])
        x = x + FFN_RESIDUAL_WEIGHT * rms_norm(f, ff1_post_g[l])
        m = token_mix(rms_norm(x, mix_pre_g[l]), w_in[l], conv_w[l], conv_b[l],
                      dt_bias_f[l], dt_bias_b[l], a_log_f[l], a_log_b[l], d_skip[l],
                      ssd_norm_g[l], gmlp_ln_g[l], gmlp_ln_b[l], spatial_w[l], spatial_b[l], w_out[l])
        x = x + rms_norm(m, mix_post_g[l])
        f = swiglu(rms_norm(x, ff2_pre_g[l]), ff2_w_gate[l], ff2_w_up[l], ff2_w_down[l])
        x = x + FFN_RESIDUAL_WEIGHT * rms_norm(f, ff2_post_g[l])
    return x
```

```python
import functools
import math

import jax
import jax.numpy as jnp
from jax import lax
from jax.experimental import pallas as pl
from jax.experimental.pallas import tpu as pltpu

F32 = jnp.float32
BF16 = jnp.bfloat16

D_MODEL = 1024
D_SSD = 1024
D_GMLP = 1024
D_MIX = D_SSD + D_GMLP
SSD_HEAD_DIM = 64
SSD_HEADS = D_SSD // SSD_HEAD_DIM
SSD_GROUPS = 4
HEADS_PER_GROUP = SSD_HEADS // SSD_GROUPS
GROUP_WIDTH = HEADS_PER_GROUP * SSD_HEAD_DIM
D_STATE = 128
CONV_WIDTH = 5
CONV_PAD = (CONV_WIDTH - 1) // 2
CONV_DIM = D_SSD + 2 * SSD_GROUPS * D_STATE
CHUNK = 128
GMLP_GROUPS = 8
GMLP_GROUP_DIM = D_GMLP // GMLP_GROUPS
D_FF = 2816
EPS = 1e-6

LANES = 128
SUBLANES = 8
HALO = SUBLANES
TOKEN_TILE = 512
FF_TILE = 512
VMEM_LIMIT = 56 * 1024 * 1024


def _rms(x, g):
    return x * lax.rsqrt(jnp.mean(x * x, axis=-1, keepdims=True) + EPS) * g


def _silu(x):
    return x * jax.nn.sigmoid(x)


def _gelu(x):
    return 0.5 * x * (1.0 + lax.erf(x * math.sqrt(0.5)))


def _softplus(x):
    return jnp.maximum(x, 0.0) + jnp.log1p(jnp.exp(-jnp.abs(x)))


def _split3(v):
    hi = v.astype(BF16)
    r1 = v - hi.astype(F32)
    mid = r1.astype(BF16)
    lo = (r1 - mid.astype(F32)).astype(BF16)
    return hi, mid, lo


def _dot(a, b):
    return jnp.dot(a, b, preferred_element_type=F32)


def _dot01_left(m01, v):
    hi, mid, lo = _split3(v)
    return _dot(m01, hi) + _dot(m01, mid) + _dot(m01, lo)


def _dot01_right(v, m01):
    hi, mid, lo = _split3(v)
    return _dot(hi, m01) + _dot(mid, m01) + _dot(lo, m01)


def _const_spec(shape):
    zeros = (0,) * len(shape)
    return pl.BlockSpec(shape, lambda *_: zeros, pipeline_mode=pl.Buffered(1))


def _ffn_kernel(x_ref, pre_g_ref, wg_ref, wu_ref, wd_ref, post_g_ref, o_ref):
    x = x_ref[...]
    h = _rms(x, pre_g_ref[...]).astype(BF16)
    acc = jnp.zeros(x.shape, F32)
    for start in range(0, D_FF, FF_TILE):
        size = min(FF_TILE, D_FF - start)
        g = _dot(h, wg_ref[:, start:start + size])
        u = _dot(h, wu_ref[:, start:start + size])
        a = (_silu(g) * u).astype(BF16)
        acc = acc + _dot(a, wd_ref[start:start + size, :])
    o_ref[...] = x + 0.5 * _rms(acc, post_g_ref[...])


def _ffn(x, pre_g, wg, wu, wd, post_g):
    t = x.shape[0]
    row = pl.BlockSpec((TOKEN_TILE, D_MODEL), lambda i: (i, 0))
    return pl.pallas_call(
        _ffn_kernel,
        grid=(t // TOKEN_TILE,),
        in_specs=[row, _const_spec((1, D_MODEL)), _const_spec((D_MODEL, D_FF)),
                  _const_spec((D_MODEL, D_FF)), _const_spec((D_FF, D_MODEL)),
                  _const_spec((1, D_MODEL))],
        out_specs=row,
        out_shape=jax.ShapeDtypeStruct((t, D_MODEL), F32),
        compiler_params=pltpu.CompilerParams(
            dimension_semantics=("parallel",), vmem_limit_bytes=VMEM_LIMIT),
        name="ffn",
    )(x, pre_g, wg, wu, wd, post_g)


def _inproj_kernel(x_ref, g_ref, wz_ref, wxbc_ref, wdt_ref, wu_ref, wv_ref,
                   dt_bias_ref, ln_g_ref, ln_b_ref,
                   z_ref, xbc_ref, dt_ref, u_ref, v_ref):
    h = _rms(x_ref[...], g_ref[...]).astype(BF16)
    z_ref[...] = _dot(h, wz_ref[...])
    xbc_ref[...] = _dot(h, wxbc_ref[...])
    dt_ref[...] = _softplus(_dot(h, wdt_ref[...]) + dt_bias_ref[...])
    u_ref[...] = _gelu(_dot(h, wu_ref[...]))
    v = _gelu(_dot(h, wv_ref[...]))
    vc = v - jnp.mean(v, axis=-1, keepdims=True)
    vn = vc * lax.rsqrt(jnp.mean(vc * vc, axis=-1, keepdims=True) + EPS)
    v_ref[...] = vn * ln_g_ref[...] + ln_b_ref[...]


def _inproj(x, g, wz, wxbc, wdt, wu, wv, dt_bias, ln_g, ln_b):
    t = x.shape[0]
    row = lambda w: pl.BlockSpec((TOKEN_TILE, w), lambda i: (i, 0))
    out = lambda w: jax.ShapeDtypeStruct((t, w), F32)
    return pl.pallas_call(
        _inproj_kernel,
        grid=(t // TOKEN_TILE,),
        in_specs=[row(D_MODEL), _const_spec((1, D_MODEL)),
                  _const_spec((D_MODEL, D_SSD)), _const_spec((D_MODEL, CONV_DIM)),
                  _const_spec((D_MODEL, LANES)), _const_spec((D_MODEL, D_GMLP)),
                  _const_spec((D_MODEL, D_GMLP)), _const_spec((1, LANES)),
                  _const_spec((1, D_GMLP)), _const_spec((1, D_GMLP))],
        out_specs=[row(D_SSD), row(CONV_DIM), row(LANES), row(D_GMLP), row(D_GMLP)],
        out_shape=[out(D_SSD), out(CONV_DIM), out(LANES), out(D_GMLP), out(D_GMLP)],
        compiler_params=pltpu.CompilerParams(
            dimension_semantics=("parallel",), vmem_limit_bytes=VMEM_LIMIT),
        name="inproj",
    )(x, g, wz, wxbc, wdt, wu, wv, dt_bias, ln_g, ln_b)


def _ssd_direction(forward, lane_off, main_ref, prev_ref, next_ref, dt_ref,
                   is_first, is_last, conv_w_ref, conv_b_ref, a_row, d_ref,
                   expand_ref, h_ref, y_ref, ext_ref):
    q = CHUNK
    ext_ref[0:HALO, :] = jnp.where(is_first, 0.0, prev_ref[...])
    ext_ref[HALO:HALO + q, :] = main_ref[...]
    ext_ref[HALO + q:HALO + q + HALO, :] = jnp.where(is_last, 0.0, next_ref[...])
    conv = conv_b_ref[...] + conv_w_ref[0:1, :] * ext_ref[HALO - CONV_PAD:HALO - CONV_PAD + q, :]
    for k in range(1, CONV_WIDTH):
        off = HALO - CONV_PAD + k
        conv = conv + conv_w_ref[k:k + 1, :] * ext_ref[off:off + q, :]
    xbc = _silu(conv)
    xs = xbc[:, :D_SSD]
    bm = xbc[:, D_SSD:D_SSD + SSD_GROUPS * D_STATE]
    cm = xbc[:, D_SSD + SSD_GROUPS * D_STATE:]

    rows = lax.broadcasted_iota(jnp.int32, (q, q), 0)
    cols = lax.broadcasted_iota(jnp.int32, (q, q), 1)
    lower = cols <= rows
    tril01 = jnp.where(lower, 1.0, 0.0).astype(BF16)
    mask = lower if forward else cols >= rows

    dtv = dt_ref[...]
    dta = dtv * a_row
    incl = _dot01_left(tril01, dta)
    total = incl[q - 1:q, :]
    pot = incl if forward else dta - incl
    pot_t = pot.T
    dt_t = dtv.T
    if forward:
        w_t = jnp.exp(pot_t[:, q - 1:q] - pot_t) * dt_t
        out_scale = jnp.exp(pot)
    else:
        w_t = jnp.exp(-pot_t) * dt_t
        out_scale = jnp.exp(total + pot)
    decay_row = jnp.exp(jnp.broadcast_to(total, (SUBLANES, LANES)))
    expand = expand_ref[:, 0:D_SSD] if forward else expand_ref[:, D_SSD:2 * D_SSD]
    decay_lanes = _dot01_right(decay_row, expand)[0:1, :]

    lane = lax.broadcasted_iota(jnp.int32, (1, GROUP_WIDTH), 1)
    for g in range(SSD_GROUPS):
        b_g = bm[:, g * D_STATE:(g + 1) * D_STATE]
        c_g = cm[:, g * D_STATE:(g + 1) * D_STATE]
        cb = lax.dot_general(c_g.astype(BF16), b_g.astype(BF16),
                             (((1,), (1,)), ((), ())), preferred_element_type=F32)
        b_t = b_g.T
        x_g = xs[:, g * GROUP_WIDTH:(g + 1) * GROUP_WIDTH]
        h_g = h_ref[g]
        x_bf = x_g.astype(BF16)
        rhs = jnp.concatenate([x_bf, h_g.astype(BF16)], axis=0)
        acc_y = jnp.zeros((q, GROUP_WIDTH), F32)
        acc_s = jnp.zeros((D_STATE, GROUP_WIDTH), F32)
        for r in range(HEADS_PER_GROUP):
            hl = lane_off + g * HEADS_PER_GROUP + r
            head_lanes = (lane >= r * SSD_HEAD_DIM) & (lane < (r + 1) * SSD_HEAD_DIM)
            seg = pot[:, hl:hl + 1] - pot_t[hl:hl + 1, :]
            m = cb * jnp.exp(jnp.where(mask, seg, -jnp.inf)) * dt_t[hl:hl + 1, :]
            c_s = c_g * out_scale[:, hl:hl + 1]
            lhs = jnp.concatenate([m, c_s], axis=1).astype(BF16)
            acc_y = acc_y + _dot(lhs, jnp.where(head_lanes, rhs, jnp.zeros_like(rhs)))
            b_w = (b_t * w_t[hl:hl + 1, :]).astype(BF16)
            acc_s = acc_s + _dot(b_w, jnp.where(head_lanes, x_bf, jnp.zeros_like(x_bf)))
        if forward:
            acc_y = acc_y + d_ref[:, g * GROUP_WIDTH:(g + 1) * GROUP_WIDTH] * x_g
        y_ref[:, g * GROUP_WIDTH:(g + 1) * GROUP_WIDTH] = acc_y
        h_ref[g] = h_g * decay_lanes[:, g * GROUP_WIDTH:(g + 1) * GROUP_WIDTH] + acc_s


def _ssd_kernel(xf_ref, xf_prev_ref, xf_next_ref, dtf_ref,
                xb_ref, xb_prev_ref, xb_next_ref, dtb_ref,
                conv_w_ref, conv_b_ref, a_log_ref, d_ref, expand_ref,
                yf_ref, yb_ref, hf_ref, hb_ref, ext_ref):
    c = pl.program_id(1)
    nc = pl.num_programs(1)

    @pl.when(c == 0)
    def _():
        hf_ref[...] = jnp.zeros_like(hf_ref)
        hb_ref[...] = jnp.zeros_like(hb_ref)

    head_lane = lax.broadcasted_iota(jnp.int32, (1, LANES), 1)
    a_row = jnp.where(head_lane < 2 * SSD_HEADS, -jnp.exp(a_log_ref[...]), 0.0)
    _ssd_direction(True, 0, xf_ref, xf_prev_ref, xf_next_ref, dtf_ref,
                   c == 0, c == nc - 1, conv_w_ref, conv_b_ref, a_row, d_ref,
                   expand_ref, hf_ref, yf_ref, ext_ref)
    _ssd_direction(False, SSD_HEADS, xb_ref, xb_prev_ref, xb_next_ref, dtb_ref,
                   c == nc - 1, c == 0, conv_w_ref, conv_b_ref, a_row, d_ref,
                   expand_ref, hb_ref, yb_ref, ext_ref)


def _ssd(xbc, dt, conv_w, conv_b, a_log_row, d_full, expand, bsz, nc):
    t = xbc.shape[0]
    halo_per_chunk = CHUNK // HALO
    last_halo = t // HALO - 1
    fwd = lambda b, c: b * nc + c
    bwd = lambda b, c: b * nc + nc - 1 - c

    def specs(chunk_of):
        main = pl.BlockSpec((CHUNK, CONV_DIM), lambda b, c: (chunk_of(b, c), 0))
        prev = pl.BlockSpec(
            (HALO, CONV_DIM),
            lambda b, c: (jnp.maximum(chunk_of(b, c) * halo_per_chunk - 1, 0), 0))
        nxt = pl.BlockSpec(
            (HALO, CONV_DIM),
            lambda b, c: (jnp.minimum((chunk_of(b, c) + 1) * halo_per_chunk, last_halo), 0))
        dts = pl.BlockSpec((CHUNK, LANES), lambda b, c: (chunk_of(b, c), 0))
        return [main, prev, nxt, dts]

    y_shape = jax.ShapeDtypeStruct((t, D_SSD), F32)
    return pl.pallas_call(
        _ssd_kernel,
        grid=(bsz, nc),
        in_specs=specs(fwd) + specs(bwd) + [
            _const_spec((SUBLANES, CONV_DIM)), _const_spec((1, CONV_DIM)),
            _const_spec((1, LANES)), _const_spec((1, D_SSD)),
            _const_spec((LANES, 2 * D_SSD))],
        out_specs=[pl.BlockSpec((CHUNK, D_SSD), lambda b, c: (fwd(b, c), 0)),
                   pl.BlockSpec((CHUNK, D_SSD), lambda b, c: (bwd(b, c), 0))],
        out_shape=[y_shape, y_shape],
        scratch_shapes=[pltpu.VMEM((SSD_GROUPS, D_STATE, GROUP_WIDTH), F32),
                        pltpu.VMEM((SSD_GROUPS, D_STATE, GROUP_WIDTH), F32),
                        pltpu.VMEM((CHUNK + 2 * HALO, CONV_DIM), F32)],
        compiler_params=pltpu.CompilerParams(
            dimension_semantics=("parallel", "arbitrary"), vmem_limit_bytes=VMEM_LIMIT),
        name="ssd",
    )(xbc, xbc, xbc, dt, xbc, xbc, xbc, dt, conv_w, conv_b, a_log_row, d_full, expand)


def _mixout_kernel(x_ref, yf_ref, yb_ref, z_ref, u_ref, v_ref, norm_g_ref,
                   sw_ref, sb_ref, wo_ssd_ref, wo_gmlp_ref, post_g_ref, o_ref, yg_ref):
    y = yf_ref[...] + yb_ref[...]
    y_ssd = _rms(y * _silu(z_ref[...]), norm_g_ref[...]).astype(BF16)
    m = _dot(y_ssd, wo_ssd_ref[...])
    for ci in range(TOKEN_TILE // CHUNK):
        rows = slice(ci * CHUNK, (ci + 1) * CHUNK)
        v = v_ref[rows, :].astype(BF16)
        mixed = jnp.concatenate(
            [_dot(sw_ref[g], v[:, g * GMLP_GROUP_DIM:(g + 1) * GMLP_GROUP_DIM])
             for g in range(GMLP_GROUPS)], axis=1)
        yg_ref[rows, :] = (u_ref[rows, :] * (mixed + sb_ref[...])).astype(BF16)
    m = m + _dot(yg_ref[...], wo_gmlp_ref[...])
    o_ref[...] = x_ref[...] + _rms(m, post_g_ref[...])


def _mixout(x, yf, yb, z, u, v, norm_g, sw, sb_full, wo_ssd, wo_gmlp, post_g):
    t = x.shape[0]
    row = pl.BlockSpec((TOKEN_TILE, D_MODEL), lambda i: (i, 0))
    return pl.pallas_call(
        _mixout_kernel,
        grid=(t // TOKEN_TILE,),
        in_specs=[row] * 6 + [
            _const_spec((1, D_SSD)), _const_spec((GMLP_GROUPS, CHUNK, CHUNK)),
            _const_spec((CHUNK, D_GMLP)), _const_spec((D_SSD, D_MODEL)),
            _const_spec((D_GMLP, D_MODEL)), _const_spec((1, D_MODEL))],
        out_specs=row,
        out_shape=jax.ShapeDtypeStruct((t, D_MODEL), F32),
        scratch_shapes=[pltpu.VMEM((TOKEN_TILE, D_GMLP), BF16)],
        compiler_params=pltpu.CompilerParams(
            dimension_semantics=("parallel",), vmem_limit_bytes=VMEM_LIMIT),
        name="mixout",
    )(x, yf, yb, z, u, v, norm_g, sw, sb_full, wo_ssd, wo_gmlp, post_g)


def _pad_lanes(row):
    return jnp.pad(row, ((0, 0), (0, LANES - row.shape[1])))


def kernel(x, ff1_pre_g, ff1_w_gate, ff1_w_up, ff1_w_down, ff1_post_g, mix_pre_g, w_in, conv_w, conv_b, dt_bias_f, dt_bias_b, a_log_f, a_log_b, d_skip, ssd_norm_g, gmlp_ln_g, gmlp_ln_b, spatial_w, spatial_b, w_out, mix_post_g, ff2_pre_g, ff2_w_gate, ff2_w_up, ff2_w_down, ff2_post_g):
    bsz, seqlen, _ = x.shape
    depth = w_in.shape[0]
    nc = seqlen // CHUNK
    t = bsz * seqlen
    assert seqlen % CHUNK == 0 and t % TOKEN_TILE == 0
    xt = x.reshape(t, D_MODEL)
    row = lambda p: p.reshape(1, -1)
    expand = (jnp.arange(LANES)[:, None]
              == (jnp.arange(2 * D_SSD) // SSD_HEAD_DIM)[None, :]).astype(BF16)
    o_z, o_xbc = D_SSD, D_SSD + CONV_DIM
    o_dt, o_u = o_xbc, o_xbc + 2 * SSD_HEADS
    o_v = o_u + D_GMLP
    for l in range(depth):
        xt = _ffn(xt, row(ff1_pre_g[l]), ff1_w_gate[l].astype(BF16), ff1_w_up[l].astype(BF16),
                  ff1_w_down[l].astype(BF16), row(ff1_post_g[l]))
        w = w_in[l]
        w_dt = jnp.pad(w[:, o_dt:o_u], ((0, 0), (0, LANES - 2 * SSD_HEADS))).astype(BF16)
        dt_bias = _pad_lanes(jnp.concatenate([dt_bias_f[l], dt_bias_b[l]]).reshape(1, -1))
        z, xbc, dt, u, v = _inproj(
            xt, row(mix_pre_g[l]), w[:, :o_z].astype(BF16), w[:, o_z:o_xbc].astype(BF16), w_dt,
            w[:, o_u:o_v].astype(BF16), w[:, o_v:].astype(BF16), dt_bias,
            row(gmlp_ln_g[l]), row(gmlp_ln_b[l]))
        conv_w_pad = jnp.pad(conv_w[l], ((0, SUBLANES - CONV_WIDTH), (0, 0)))
        a_log_row = _pad_lanes(jnp.concatenate([a_log_f[l], a_log_b[l]]).reshape(1, -1))
        d_full = jnp.repeat(d_skip[l], SSD_HEAD_DIM).reshape(1, D_SSD)
        yf, yb = _ssd(xbc, dt, conv_w_pad, row(conv_b[l]), a_log_row, d_full, expand, bsz, nc)
        sb_full = jnp.repeat(spatial_b[l].T, GMLP_GROUP_DIM, axis=1)
        xt = _mixout(xt, yf, yb, z, u, v, row(ssd_norm_g[l]), spatial_w[l].astype(BF16), sb_full,
                     w_out[l, :D_SSD].astype(BF16), w_out[l, D_SSD:].astype(BF16), row(mix_post_g[l]))
        xt = _ffn(xt, row(ff2_pre_g[l]), ff2_w_gate[l].astype(BF16), ff2_w_up[l].astype(BF16),
                  ff2_w_down[l].astype(BF16), row(ff2_post_g[l]))
    return xt.reshape(bsz, seqlen, D_MODEL)
```

```python
import functools
import math

import jax
import jax.numpy as jnp
from jax import lax
from jax.experimental import pallas as pl
from jax.experimental.pallas import tpu as pltpu

F32 = jnp.float32
BF16 = jnp.bfloat16

D_MODEL = 1024
D_SSD = 1024
D_GMLP = 1024
SSD_HEAD_DIM = 64
SSD_HEADS = D_SSD // SSD_HEAD_DIM
SSD_GROUPS = 4
HEADS_PER_GROUP = SSD_HEADS // SSD_GROUPS
D_STATE = 128
CONV_WIDTH = 5
CONV_PAD = (CONV_WIDTH - 1) // 2
CONV_DIM = D_SSD + 2 * SSD_GROUPS * D_STATE
CHUNK = 128
GMLP_GROUPS = 8
GMLP_GROUP_DIM = D_GMLP // GMLP_GROUPS
D_FF = 2816
EPS = 1e-6
LOG2E = math.log2(math.e)

LANES = 128
SUBLANES = 8
PAD_HEAD = LANES
HALO = 16
TOKEN_TILE = 512
FF_TILE = 512
CONV_TILE = 512
VMEM_LIMIT = 56 * 1024 * 1024


def _rms(x, g):
    return x * lax.rsqrt(jnp.mean(x * x, axis=-1, keepdims=True) + EPS) * g


def _silu(x):
    return x * jax.nn.sigmoid(x)


def _gelu(x):
    return 0.5 * x * (1.0 + lax.erf(x * math.sqrt(0.5)))


def _softplus(x):
    return jnp.maximum(x, 0.0) + jnp.log1p(jnp.exp(-jnp.abs(x)))


def _split3(v):
    hi = v.astype(BF16)
    r1 = v - hi.astype(F32)
    mid = r1.astype(BF16)
    lo = (r1 - mid.astype(F32)).astype(BF16)
    return hi, mid, lo


def _dot(a, b):
    return jnp.dot(a, b, preferred_element_type=F32)


def _dot01_left(m01, v):
    hi, mid, lo = _split3(v)
    return _dot(m01, hi) + _dot(m01, mid) + _dot(m01, lo)


def _dot01_right(v, m01):
    hi, mid, lo = _split3(v)
    return _dot(hi, m01) + _dot(mid, m01) + _dot(lo, m01)


def _tri_mask(n, lower):
    rows = lax.broadcasted_iota(jnp.int32, (n, n), 0)
    cols = lax.broadcasted_iota(jnp.int32, (n, n), 1)
    return cols <= rows if lower else cols >= rows


def _const_spec(shape):
    zeros = (0,) * len(shape)
    return pl.BlockSpec(shape, lambda *_: zeros, pipeline_mode=pl.Buffered(1))


def _ffn_kernel(x_ref, pre_g_ref, wg_ref, wu_ref, wd_ref, post_g_ref, o_ref):
    x = x_ref[...]
    h = _rms(x, pre_g_ref[...]).astype(BF16)
    acc = jnp.zeros(x.shape, F32)
    for start in range(0, D_FF, FF_TILE):
        size = min(FF_TILE, D_FF - start)
        g = _dot(h, wg_ref[:, start:start + size])
        u = _dot(h, wu_ref[:, start:start + size])
        a = (_silu(g) * u).astype(BF16)
        acc = acc + _dot(a, wd_ref[start:start + size, :])
    o_ref[...] = x + 0.5 * _rms(acc, post_g_ref[...])


def _ffn(x, pre_g, wg, wu, wd, post_g):
    t = x.shape[0]
    row = pl.BlockSpec((TOKEN_TILE, D_MODEL), lambda i: (i, 0))
    return pl.pallas_call(
        _ffn_kernel,
        grid=(t // TOKEN_TILE,),
        in_specs=[row, _const_spec((1, D_MODEL)), _const_spec((D_MODEL, D_FF)),
                  _const_spec((D_MODEL, D_FF)), _const_spec((D_FF, D_MODEL)),
                  _const_spec((1, D_MODEL))],
        out_specs=row,
        out_shape=jax.ShapeDtypeStruct((t, D_MODEL), F32),
        compiler_params=pltpu.CompilerParams(
            dimension_semantics=("parallel",), vmem_limit_bytes=VMEM_LIMIT),
        name="ffn",
    )(x, pre_g, wg, wu, wd, post_g)


def _inproj_kernel(x_ref, x_prev_ref, x_next_ref, g_ref, wz_ref, wxbc_ref, wdt_ref, wu_ref,
                   wv_ref, dt_bias_ref, a_log_ref, ln_g_ref, ln_b_ref, conv_w_ref, conv_b_ref,
                   z_ref, xbc_ref, bt_ref, dt_ref, pot_ref, dtt_ref, pott_ref, u_ref, v_ref,
                   ext_ref, *, tiles_per_seq):
    i = pl.program_id(0)
    first = (i % tiles_per_seq) == 0
    last = (i % tiles_per_seq) == tiles_per_seq - 1
    g = g_ref[...]
    h = _rms(x_ref[...], g).astype(BF16)
    h_ext = jnp.concatenate([_rms(x_prev_ref[...], g).astype(BF16), h,
                             _rms(x_next_ref[...], g).astype(BF16)], axis=0)
    n = TOKEN_TILE
    chunks = TOKEN_TILE // CHUNK
    b_cols = D_SSD
    for start in range(0, CONV_DIM, CONV_TILE):
        cols = slice(start, start + CONV_TILE)
        xe = _dot(h_ext, wxbc_ref[:, cols])
        ext_ref[0:HALO, :] = jnp.where(first, 0.0, xe[0:HALO])
        ext_ref[HALO:HALO + n, :] = xe[HALO:HALO + n]
        ext_ref[HALO + n:, :] = jnp.where(last, 0.0, xe[HALO + n:])
        conv = conv_b_ref[:, cols] + conv_w_ref[0:1, cols] * ext_ref[HALO - CONV_PAD:HALO - CONV_PAD + n, :]
        for k in range(1, CONV_WIDTH):
            off = HALO - CONV_PAD + k
            conv = conv + conv_w_ref[k:k + 1, cols] * ext_ref[off:off + n, :]
        act = _silu(conv)
        xbc_ref[:, cols] = act.astype(BF16)
        if start == b_cols:
            for c in range(chunks):
                for grp in range(SSD_GROUPS):
                    blk = act[c * CHUNK:(c + 1) * CHUNK, grp * D_STATE:(grp + 1) * D_STATE]
                    r0 = (c * SSD_GROUPS + grp) * D_STATE
                    bt_ref[r0:r0 + D_STATE, :] = blk.T.astype(BF16)

    z_ref[...] = _dot(h, wz_ref[...]).astype(BF16)
    dt = _softplus(_dot(h, wdt_ref[...]) + dt_bias_ref[...])
    dt_ref[...] = dt
    head_lane = lax.broadcasted_iota(jnp.int32, (1, LANES), 1)
    a_row = jnp.where(head_lane < 2 * SSD_HEADS, -jnp.exp(a_log_ref[...]), 0.0)
    tril01 = jnp.where(_tri_mask(CHUNK, True), 1.0, 0.0).astype(BF16)
    for c in range(chunks):
        rows = slice(c * CHUNK, (c + 1) * CHUNK)
        dta = dt[rows, :] * a_row
        incl = _dot01_left(tril01, dta)
        pot = jnp.where(head_lane < SSD_HEADS, incl, dta - incl)
        pot_ref[rows, :] = pot
        hrows = slice(c * 2 * SSD_HEADS, (c + 1) * 2 * SSD_HEADS)
        pott_ref[hrows, :] = pot.T[0:2 * SSD_HEADS, :]
        dtt_ref[hrows, :] = dt[rows, :].T[0:2 * SSD_HEADS, :]
    u_ref[...] = _gelu(_dot(h, wu_ref[...])).astype(BF16)
    v = _gelu(_dot(h, wv_ref[...]))
    vc = v - jnp.mean(v, axis=-1, keepdims=True)
    vn = vc * lax.rsqrt(jnp.mean(vc * vc, axis=-1, keepdims=True) + EPS)
    v_ref[...] = (vn * ln_g_ref[...] + ln_b_ref[...]).astype(BF16)


def _inproj(x, g, wz, wxbc, wdt, wu, wv, dt_bias, a_log_row, ln_g, ln_b, conv_w, conv_b, seqlen):
    t = x.shape[0]
    chunks = TOKEN_TILE // CHUNK
    halo_per_tile = TOKEN_TILE // HALO
    last_halo = t // HALO - 1
    row = lambda w, n=TOKEN_TILE: pl.BlockSpec((n, w), lambda i: (i, 0))
    prev = pl.BlockSpec((HALO, D_MODEL), lambda i: (jnp.maximum(i * halo_per_tile - 1, 0), 0))
    nxt = pl.BlockSpec((HALO, D_MODEL),
                       lambda i: (jnp.minimum((i + 1) * halo_per_tile, last_halo), 0))
    out = lambda w, dt, n=t: jax.ShapeDtypeStruct((n, w), dt)
    bt_rows = SSD_GROUPS * D_STATE
    hd_rows = 2 * SSD_HEADS
    return pl.pallas_call(
        functools.partial(_inproj_kernel, tiles_per_seq=seqlen // TOKEN_TILE),
        grid=(t // TOKEN_TILE,),
        in_specs=[row(D_MODEL), prev, nxt, _const_spec((1, D_MODEL)),
                  _const_spec((D_MODEL, D_SSD)), _const_spec((D_MODEL, CONV_DIM)),
                  _const_spec((D_MODEL, LANES)), _const_spec((D_MODEL, D_GMLP)),
                  _const_spec((D_MODEL, D_GMLP)), _const_spec((1, LANES)),
                  _const_spec((1, LANES)), _const_spec((1, D_GMLP)), _const_spec((1, D_GMLP)),
                  _const_spec((SUBLANES, CONV_DIM)), _const_spec((1, CONV_DIM))],
        out_specs=[row(D_SSD), row(CONV_DIM), row(CHUNK, chunks * bt_rows), row(LANES),
                   row(LANES), row(CHUNK, chunks * hd_rows), row(CHUNK, chunks * hd_rows),
                   row(D_GMLP), row(D_GMLP)],
        out_shape=[out(D_SSD, BF16), out(CONV_DIM, BF16),
                   out(CHUNK, BF16, t // CHUNK * bt_rows), out(LANES, F32), out(LANES, F32),
                   out(CHUNK, F32, t // CHUNK * hd_rows), out(CHUNK, F32, t // CHUNK * hd_rows),
                   out(D_GMLP, BF16), out(D_GMLP, BF16)],
        scratch_shapes=[pltpu.VMEM((TOKEN_TILE + 2 * HALO, CONV_TILE), F32)],
        compiler_params=pltpu.CompilerParams(
            dimension_semantics=("parallel",), vmem_limit_bytes=VMEM_LIMIT),
        name="inproj",
    )(x, x, x, g, wz, wxbc, wdt, wu, wv, dt_bias, a_log_row, ln_g, ln_b, conv_w, conv_b)


def _ssd_direction(forward, lane_off, xbc_ref, bt_ref, dt_ref, pot_ref, dtt_ref, pott_ref,
                   a_row, d_ref, expand_ref, h_ref, y_ref):
    q = CHUNK
    mask = _tri_mask(q, forward)
    dtv = dt_ref[...]
    pot = pot_ref[...]
    dt_t = dtt_ref[lane_off:lane_off + SSD_HEADS, :]
    pot_t = pott_ref[lane_off:lane_off + SSD_HEADS, :]
    last = pot[q - 1:q, :]
    head_lane = lax.broadcasted_iota(jnp.int32, (1, LANES), 1)
    own = (head_lane >= lane_off) & (head_lane < lane_off + SSD_HEADS)
    total = jnp.where(own, last if forward else dtv[q - 1:q, :] * a_row - last, 0.0)
    if forward:
        w_t = jnp.exp(pot_t[:, q - 1:q] - pot_t) * dt_t
        out_scale = jnp.exp(pot)
    else:
        w_t = jnp.exp(-pot_t) * dt_t
        out_scale = jnp.exp(total + pot)
    pot2 = pot * LOG2E
    pot2_t = (pot_t - jnp.log(dt_t)) * LOG2E
    decay_row = jnp.exp(jnp.broadcast_to(total, (SUBLANES, LANES)))
    e0 = 0 if forward else SSD_HEADS * PAD_HEAD
    decay_lanes = _dot01_right(decay_row, expand_ref[:, e0:e0 + SSD_HEADS * PAD_HEAD])[0:1, :]

    low_half = lax.broadcasted_iota(jnp.int32, (1, LANES), 1) < SSD_HEAD_DIM
    for pair in range(SSD_HEADS // 2):
        grp = pair // (HEADS_PER_GROUP // 2)
        if pair % (HEADS_PER_GROUP // 2) == 0:
            b_g = xbc_ref[:, D_SSD + grp * D_STATE:D_SSD + (grp + 1) * D_STATE]
            c_g = xbc_ref[:, D_SSD + (SSD_GROUPS + grp) * D_STATE:
                          D_SSD + (SSD_GROUPS + grp + 1) * D_STATE]
            cb = lax.dot_general(c_g, b_g, (((1,), (1,)), ((), ())),
                                 preferred_element_type=F32).astype(BF16)
            c_f = c_g.astype(F32)
            b_t = bt_ref[grp * D_STATE:(grp + 1) * D_STATE, :].astype(F32)
        x_f = xbc_ref[:, pair * LANES:(pair + 1) * LANES].astype(F32)
        x_pad = (jnp.where(low_half, x_f, 0.0).astype(BF16),
                 jnp.where(low_half, pltpu.roll(x_f, SSD_HEAD_DIM, 1), 0.0).astype(BF16))
        ys = []
        for j in range(2):
            hd = 2 * pair + j
            hl = lane_off + hd
            seg = pot2[:, hl:hl + 1] - pot2_t[hd:hd + 1, :]
            m = cb * jnp.exp2(jnp.where(mask, seg, -jnp.inf)).astype(BF16)
            c_s = (c_f * out_scale[:, hl:hl + 1]).astype(BF16)
            h_h = h_ref[hd]
            lhs = jnp.concatenate([m, c_s], axis=1)
            rhs = jnp.concatenate([x_pad[j], h_h.astype(BF16)], axis=0)
            ys.append(_dot(lhs, rhs))
            b_w = (b_t * w_t[hd:hd + 1, :]).astype(BF16)
            h_ref[hd] = (h_h * decay_lanes[:, hd * PAD_HEAD:(hd + 1) * PAD_HEAD]
                         + _dot(b_w, x_pad[j]))
        y_pair = ys[0] + pltpu.roll(ys[1], SSD_HEAD_DIM, 1)
        if forward:
            y_pair = y_pair + d_ref[:, pair * LANES:(pair + 1) * LANES] * x_f
        y_ref[:, pair * LANES:(pair + 1) * LANES] = y_pair.astype(BF16)


def _ssd_kernel(xf_ref, btf_ref, dtf_ref, potf_ref, dttf_ref, pottf_ref,
                xb_ref, btb_ref, dtb_ref, potb_ref, dttb_ref, pottb_ref,
                a_log_ref, d_ref, expand_ref, yf_ref, yb_ref, hf_ref, hb_ref):
    c = pl.program_id(1)

    @pl.when(c == 0)
    def _():
        hf_ref[...] = jnp.zeros_like(hf_ref)
        hb_ref[...] = jnp.zeros_like(hb_ref)

    head_lane = lax.broadcasted_iota(jnp.int32, (1, LANES), 1)
    a_row = jnp.where(head_lane < 2 * SSD_HEADS, -jnp.exp(a_log_ref[...]), 0.0)
    _ssd_direction(True, 0, xf_ref, btf_ref, dtf_ref, potf_ref, dttf_ref, pottf_ref,
                   a_row, d_ref, expand_ref, hf_ref, yf_ref)
    _ssd_direction(False, SSD_HEADS, xb_ref, btb_ref, dtb_ref, potb_ref, dttb_ref, pottb_ref,
                   a_row, d_ref, expand_ref, hb_ref, yb_ref)


def _ssd(xbc, bt, dt, pot, dtt, pott, a_log_row, d_full, expand, bsz, nc):
    t = xbc.shape[0]
    fwd = lambda b, c: (b * nc + c, 0)
    bwd = lambda b, c: (b * nc + nc - 1 - c, 0)
    specs = lambda idx: [pl.BlockSpec((CHUNK, CONV_DIM), idx),
                         pl.BlockSpec((SSD_GROUPS * D_STATE, CHUNK), idx),
                         pl.BlockSpec((CHUNK, LANES), idx), pl.BlockSpec((CHUNK, LANES), idx),
                         pl.BlockSpec((2 * SSD_HEADS, CHUNK), idx),
                         pl.BlockSpec((2 * SSD_HEADS, CHUNK), idx)]
    y_shape = jax.ShapeDtypeStruct((t, D_SSD), BF16)
    state = pltpu.VMEM((SSD_HEADS, D_STATE, PAD_HEAD), F32)
    return pl.pallas_call(
        _ssd_kernel,
        grid=(bsz, nc),
        in_specs=specs(fwd) + specs(bwd) + [
            _const_spec((1, LANES)), _const_spec((1, D_SSD)),
            _const_spec((LANES, 2 * SSD_HEADS * PAD_HEAD))],
        out_specs=[pl.BlockSpec((CHUNK, D_SSD), fwd), pl.BlockSpec((CHUNK, D_SSD), bwd)],
        out_shape=[y_shape, y_shape],
        scratch_shapes=[state, state],
        compiler_params=pltpu.CompilerParams(
            dimension_semantics=("parallel", "arbitrary"), vmem_limit_bytes=VMEM_LIMIT),
        name="ssd",
    )(xbc, bt, dt, pot, dtt, pott, xbc, bt, dt, pot, dtt, pott, a_log_row, d_full, expand)


def _mixout_kernel(x_ref, yf_ref, yb_ref, z_ref, u_ref, v_ref, norm_g_ref,
                   sw_ref, sb_ref, wo_ssd_ref, wo_gmlp_ref, post_g_ref, o_ref, yg_ref):
    y = yf_ref[...].astype(F32) + yb_ref[...].astype(F32)
    y_ssd = _rms(y * _silu(z_ref[...].astype(F32)), norm_g_ref[...]).astype(BF16)
    m = _dot(y_ssd, wo_ssd_ref[...])
    for ci in range(TOKEN_TILE // CHUNK):
        rows = slice(ci * CHUNK, (ci + 1) * CHUNK)
        v = v_ref[rows, :]
        mixed = jnp.concatenate(
            [_dot(sw_ref[g], v[:, g * GMLP_GROUP_DIM:(g + 1) * GMLP_GROUP_DIM])
             for g in range(GMLP_GROUPS)], axis=1)
        yg_ref[rows, :] = (u_ref[rows, :].astype(F32) * (mixed + sb_ref[...])).astype(BF16)
    m = m + _dot(yg_ref[...], wo_gmlp_ref[...])
    o_ref[...] = x_ref[...] + _rms(m, post_g_ref[...])


def _mixout(x, yf, yb, z, u, v, norm_g, sw, sb_full, wo_ssd, wo_gmlp, post_g):
    t = x.shape[0]
    row = pl.BlockSpec((TOKEN_TILE, D_MODEL), lambda i: (i, 0))
    return pl.pallas_call(
        _mixout_kernel,
        grid=(t // TOKEN_TILE,),
        in_specs=[row] * 6 + [
            _const_spec((1, D_SSD)), _const_spec((GMLP_GROUPS, CHUNK, CHUNK)),
            _const_spec((CHUNK, D_GMLP)), _const_spec((D_SSD, D_MODEL)),
            _const_spec((D_GMLP, D_MODEL)), _const_spec((1, D_MODEL))],
        out_specs=row,
        out_shape=jax.ShapeDtypeStruct((t, D_MODEL), F32),
        scratch_shapes=[pltpu.VMEM((TOKEN_TILE, D_GMLP), BF16)],
        compiler_params=pltpu.CompilerParams(
            dimension_semantics=("parallel",), vmem_limit_bytes=VMEM_LIMIT),
        name="mixout",
    )(x, yf, yb, z, u, v, norm_g, sw, sb_full, wo_ssd, wo_gmlp, post_g)


def _pad_lanes(row):
    return jnp.pad(row, ((0, 0), (0, LANES - row.shape[1])))


def kernel(x, ff1_pre_g, ff1_w_gate, ff1_w_up, ff1_w_down, ff1_post_g, mix_pre_g, w_in, conv_w, conv_b, dt_bias_f, dt_bias_b, a_log_f, a_log_b, d_skip, ssd_norm_g, gmlp_ln_g, gmlp_ln_b, spatial_w, spatial_b, w_out, mix_post_g, ff2_pre_g, ff2_w_gate, ff2_w_up, ff2_w_down, ff2_post_g):
    bsz, seqlen, _ = x.shape
    depth = w_in.shape[0]
    nc = seqlen // CHUNK
    t = bsz * seqlen
    assert seqlen % TOKEN_TILE == 0 and TOKEN_TILE % CHUNK == 0
    xt = x.reshape(t, D_MODEL)
    row = lambda p: p.reshape(1, -1)
    expand = (jnp.arange(LANES)[:, None]
              == (jnp.arange(2 * SSD_HEADS * PAD_HEAD) // PAD_HEAD)[None, :]).astype(BF16)
    o_z, o_xbc = D_SSD, D_SSD + CONV_DIM
    o_dt, o_u = o_xbc, o_xbc + 2 * SSD_HEADS
    o_v = o_u + D_GMLP
    for l in range(depth):
        xt = _ffn(xt, row(ff1_pre_g[l]), ff1_w_gate[l].astype(BF16), ff1_w_up[l].astype(BF16),
                  ff1_w_down[l].astype(BF16), row(ff1_post_g[l]))
        w = w_in[l]
        w_dt = jnp.pad(w[:, o_dt:o_u], ((0, 0), (0, LANES - 2 * SSD_HEADS))).astype(BF16)
        dt_bias = _pad_lanes(jnp.concatenate([dt_bias_f[l], dt_bias_b[l]]).reshape(1, -1))
        a_log_row = _pad_lanes(jnp.concatenate([a_log_f[l], a_log_b[l]]).reshape(1, -1))
        conv_w_pad = jnp.pad(conv_w[l], ((0, SUBLANES - CONV_WIDTH), (0, 0)))
        z, xbc, bt, dt, pot, dtt, pott, u, v = _inproj(
            xt, row(mix_pre_g[l]), w[:, :o_z].astype(BF16), w[:, o_z:o_xbc].astype(BF16), w_dt,
            w[:, o_u:o_v].astype(BF16), w[:, o_v:].astype(BF16), dt_bias, a_log_row,
            row(gmlp_ln_g[l]), row(gmlp_ln_b[l]), conv_w_pad, row(conv_b[l]), seqlen)
        d_full = jnp.repeat(d_skip[l], SSD_HEAD_DIM).reshape(1, D_SSD)
        yf, yb = _ssd(xbc, bt, dt, pot, dtt, pott, a_log_row, d_full, expand, bsz, nc)
        sb_full = jnp.repeat(spatial_b[l].T, GMLP_GROUP_DIM, axis=1)
        xt = _mixout(xt, yf, yb, z, u, v, row(ssd_norm_g[l]), spatial_w[l].astype(BF16), sb_full,
                     w_out[l, :D_SSD].astype(BF16), w_out[l, D_SSD:].astype(BF16), row(mix_post_g[l]))
        xt = _ffn(xt, row(ff2_pre_g[l]), ff2_w_gate[l].astype(BF16), ff2_w_up[l].astype(BF16),
                  ff2_w_down[l].astype(BF16), row(ff2_post_g[l]))
    return xt.reshape(bsz, seqlen, D_MODEL)
```

```python
import functools
import math

import jax
import jax.numpy as jnp
from jax import lax
from jax.experimental import pallas as pl
from jax.experimental.pallas import tpu as pltpu

F32 = jnp.float32
BF16 = jnp.bfloat16

D_MODEL = 1024
D_SSD = 1024
D_GMLP = 1024
SSD_HEAD_DIM = 64
SSD_HEADS = D_SSD // SSD_HEAD_DIM
SSD_GROUPS = 4
HEADS_PER_GROUP = SSD_HEADS // SSD_GROUPS
D_STATE = 128
CONV_WIDTH = 5
CONV_PAD = (CONV_WIDTH - 1) // 2
CONV_DIM = D_SSD + 2 * SSD_GROUPS * D_STATE
CHUNK = 128
GMLP_GROUPS = 8
GMLP_GROUP_DIM = D_GMLP // GMLP_GROUPS
D_FF = 2816
EPS = 1e-6
LOG2E = math.log2(math.e)

LANES = 128
SUBLANES = 8
PAD_HEAD = LANES
HALO = 16
TOKEN_TILE = 512
FF_TILE = 512
CONV_TILE = 512
VMEM_LIMIT = 56 * 1024 * 1024


def _rms(x, g):
    return x * lax.rsqrt(jnp.mean(x * x, axis=-1, keepdims=True) + EPS) * g


def _silu(x):
    return x * jax.nn.sigmoid(x)


def _gelu(x):
    return 0.5 * x * (1.0 + lax.erf(x * math.sqrt(0.5)))


def _softplus(x):
    return jnp.maximum(x, 0.0) + jnp.log1p(jnp.exp(-jnp.abs(x)))


def _split3(v):
    hi = v.astype(BF16)
    r1 = v - hi.astype(F32)
    mid = r1.astype(BF16)
    lo = (r1 - mid.astype(F32)).astype(BF16)
    return hi, mid, lo


def _dot(a, b):
    return jnp.dot(a, b, preferred_element_type=F32)


def _dot01_left(m01, v):
    hi, mid, lo = _split3(v)
    return _dot(m01, hi) + _dot(m01, mid) + _dot(m01, lo)


def _dot01_right(v, m01):
    hi, mid, lo = _split3(v)
    return _dot(hi, m01) + _dot(mid, m01) + _dot(lo, m01)


def _tri_mask(n, lower):
    rows = lax.broadcasted_iota(jnp.int32, (n, n), 0)
    cols = lax.broadcasted_iota(jnp.int32, (n, n), 1)
    return cols <= rows if lower else cols >= rows


def _const_spec(shape):
    zeros = (0,) * len(shape)
    return pl.BlockSpec(shape, lambda *_: zeros, pipeline_mode=pl.Buffered(1))


def _ffn_kernel(x_ref, pre_g_ref, wg_ref, wu_ref, wd_ref, post_g_ref, o_ref):
    x = x_ref[...]
    h = _rms(x, pre_g_ref[...]).astype(BF16)
    acc = jnp.zeros(x.shape, F32)
    for start in range(0, D_FF, FF_TILE):
        size = min(FF_TILE, D_FF - start)
        g = _dot(h, wg_ref[:, start:start + size])
        u = _dot(h, wu_ref[:, start:start + size])
        a = (_silu(g) * u).astype(BF16)
        acc = acc + _dot(a, wd_ref[start:start + size, :])
    o_ref[...] = x + 0.5 * _rms(acc, post_g_ref[...])


def _ffn(x, pre_g, wg, wu, wd, post_g):
    t = x.shape[0]
    row = pl.BlockSpec((TOKEN_TILE, D_MODEL), lambda i: (i, 0))
    return pl.pallas_call(
        _ffn_kernel,
        grid=(t // TOKEN_TILE,),
        in_specs=[row, _const_spec((1, D_MODEL)), _const_spec((D_MODEL, D_FF)),
                  _const_spec((D_MODEL, D_FF)), _const_spec((D_FF, D_MODEL)),
                  _const_spec((1, D_MODEL))],
        out_specs=row,
        out_shape=jax.ShapeDtypeStruct((t, D_MODEL), F32),
        compiler_params=pltpu.CompilerParams(
            dimension_semantics=("parallel",), vmem_limit_bytes=VMEM_LIMIT),
        name="ffn",
    )(x, pre_g, wg, wu, wd, post_g)


def _inproj_kernel(x_ref, x_prev_ref, x_next_ref, g_ref, wz_ref, wxbc_ref, wdt_ref, wu_ref,
                   wv_ref, dt_bias_ref, a_log_ref, ln_g_ref, ln_b_ref, conv_w_ref, conv_b_ref,
                   z_ref, xbc_ref, bt_ref, pot_ref, dtt_ref, pott_ref, u_ref, v_ref,
                   ext_ref, *, tiles_per_seq):
    i = pl.program_id(0)
    first = (i % tiles_per_seq) == 0
    last = (i % tiles_per_seq) == tiles_per_seq - 1
    g = g_ref[...]
    h = _rms(x_ref[...], g).astype(BF16)
    h_ext = jnp.concatenate([_rms(x_prev_ref[...], g).astype(BF16), h,
                             _rms(x_next_ref[...], g).astype(BF16)], axis=0)
    n = TOKEN_TILE
    chunks = TOKEN_TILE // CHUNK
    b_cols = D_SSD
    for start in range(0, CONV_DIM, CONV_TILE):
        cols = slice(start, start + CONV_TILE)
        xe = _dot(h_ext, wxbc_ref[:, cols])
        ext_ref[0:HALO, :] = jnp.where(first, 0.0, xe[0:HALO])
        ext_ref[HALO:HALO + n, :] = xe[HALO:HALO + n]
        ext_ref[HALO + n:, :] = jnp.where(last, 0.0, xe[HALO + n:])
        conv = conv_b_ref[:, cols] + conv_w_ref[0:1, cols] * ext_ref[HALO - CONV_PAD:HALO - CONV_PAD + n, :]
        for k in range(1, CONV_WIDTH):
            off = HALO - CONV_PAD + k
            conv = conv + conv_w_ref[k:k + 1, cols] * ext_ref[off:off + n, :]
        act = _silu(conv)
        xbc_ref[:, cols] = act.astype(BF16)
        if start == b_cols:
            for c in range(chunks):
                for grp in range(SSD_GROUPS):
                    blk = act[c * CHUNK:(c + 1) * CHUNK, grp * D_STATE:(grp + 1) * D_STATE]
                    r0 = (c * SSD_GROUPS + grp) * D_STATE
                    bt_ref[r0:r0 + D_STATE, :] = blk.T.astype(BF16)

    z_ref[...] = _dot(h, wz_ref[...]).astype(BF16)
    dt = _softplus(_dot(h, wdt_ref[...]) + dt_bias_ref[...])
    head_lane = lax.broadcasted_iota(jnp.int32, (1, LANES), 1)
    a_row = jnp.where(head_lane < 2 * SSD_HEADS, -jnp.exp(a_log_ref[...]), 0.0)
    tril01 = jnp.where(_tri_mask(CHUNK, True), 1.0, 0.0).astype(BF16)
    for c in range(chunks):
        rows = slice(c * CHUNK, (c + 1) * CHUNK)
        dta = dt[rows, :] * a_row
        incl = _dot01_left(tril01, dta)
        pot = jnp.where(head_lane < SSD_HEADS, incl, incl[CHUNK - 1:CHUNK, :] + dta - incl)
        pot_ref[rows, :] = pot
        hrows = slice(c * 2 * SSD_HEADS, (c + 1) * 2 * SSD_HEADS)
        pott_ref[hrows, :] = pot.T[0:2 * SSD_HEADS, :]
        dtt_ref[hrows, :] = dt[rows, :].T[0:2 * SSD_HEADS, :]
    u_ref[...] = _gelu(_dot(h, wu_ref[...])).astype(BF16)
    v = _gelu(_dot(h, wv_ref[...]))
    vc = v - jnp.mean(v, axis=-1, keepdims=True)
    vn = vc * lax.rsqrt(jnp.mean(vc * vc, axis=-1, keepdims=True) + EPS)
    v_ref[...] = (vn * ln_g_ref[...] + ln_b_ref[...]).astype(BF16)


def _inproj(x, g, wz, wxbc, wdt, wu, wv, dt_bias, a_log_row, ln_g, ln_b, conv_w, conv_b, seqlen):
    t = x.shape[0]
    chunks = TOKEN_TILE // CHUNK
    halo_per_tile = TOKEN_TILE // HALO
    last_halo = t // HALO - 1
    row = lambda w, n=TOKEN_TILE: pl.BlockSpec((n, w), lambda i: (i, 0))
    prev = pl.BlockSpec((HALO, D_MODEL), lambda i: (jnp.maximum(i * halo_per_tile - 1, 0), 0))
    nxt = pl.BlockSpec((HALO, D_MODEL),
                       lambda i: (jnp.minimum((i + 1) * halo_per_tile, last_halo), 0))
    out = lambda w, dt, n=t: jax.ShapeDtypeStruct((n, w), dt)
    bt_rows = SSD_GROUPS * D_STATE
    hd_rows = 2 * SSD_HEADS
    return pl.pallas_call(
        functools.partial(_inproj_kernel, tiles_per_seq=seqlen // TOKEN_TILE),
        grid=(t // TOKEN_TILE,),
        in_specs=[row(D_MODEL), prev, nxt, _const_spec((1, D_MODEL)),
                  _const_spec((D_MODEL, D_SSD)), _const_spec((D_MODEL, CONV_DIM)),
                  _const_spec((D_MODEL, LANES)), _const_spec((D_MODEL, D_GMLP)),
                  _const_spec((D_MODEL, D_GMLP)), _const_spec((1, LANES)),
                  _const_spec((1, LANES)), _const_spec((1, D_GMLP)), _const_spec((1, D_GMLP)),
                  _const_spec((SUBLANES, CONV_DIM)), _const_spec((1, CONV_DIM))],
        out_specs=[row(D_SSD), row(CONV_DIM), row(CHUNK, chunks * bt_rows), row(LANES),
                   row(CHUNK, chunks * hd_rows), row(CHUNK, chunks * hd_rows),
                   row(D_GMLP), row(D_GMLP)],
        out_shape=[out(D_SSD, BF16), out(CONV_DIM, BF16),
                   out(CHUNK, BF16, t // CHUNK * bt_rows), out(LANES, F32),
                   out(CHUNK, F32, t // CHUNK * hd_rows), out(CHUNK, F32, t // CHUNK * hd_rows),
                   out(D_GMLP, BF16), out(D_GMLP, BF16)],
        scratch_shapes=[pltpu.VMEM((TOKEN_TILE + 2 * HALO, CONV_TILE), F32)],
        compiler_params=pltpu.CompilerParams(
            dimension_semantics=("parallel",), vmem_limit_bytes=VMEM_LIMIT),
        name="inproj",
    )(x, x, x, g, wz, wxbc, wdt, wu, wv, dt_bias, a_log_row, ln_g, ln_b, conv_w, conv_b)


def _ssd_direction(forward, lane_off, xbc_ref, bt_ref, pot_ref, dtt_ref, pott_ref,
                   d_ref, expand_ref, h_ref, y_ref):
    q = CHUNK
    mask = _tri_mask(q, forward)
    pot = pot_ref[...]
    dt_t = dtt_ref[lane_off:lane_off + SSD_HEADS, :]
    pot_t = pott_ref[lane_off:lane_off + SSD_HEADS, :]
    far = q - 1 if forward else 0
    head_lane = lax.broadcasted_iota(jnp.int32, (1, LANES), 1)
    own = (head_lane >= lane_off) & (head_lane < lane_off + SSD_HEADS)
    total = jnp.where(own, pot[far:far + 1, :], 0.0)
    w_t = jnp.exp(pot_t[:, far:far + 1] - pot_t) * dt_t
    pot2 = pot * LOG2E
    pot2_t = (pot_t - jnp.log(dt_t)) * LOG2E
    decay_row = jnp.exp(jnp.broadcast_to(total, (SUBLANES, LANES)))
    e0 = 0 if forward else SSD_HEADS * PAD_HEAD
    decay_lanes = _dot01_right(decay_row, expand_ref[:, e0:e0 + SSD_HEADS * PAD_HEAD])[0:1, :]

    low_half = lax.broadcasted_iota(jnp.int32, (1, LANES), 1) < SSD_HEAD_DIM
    for pair in range(SSD_HEADS // 2):
        grp = pair // (HEADS_PER_GROUP // 2)
        if pair % (HEADS_PER_GROUP // 2) == 0:
            b_g = xbc_ref[:, D_SSD + grp * D_STATE:D_SSD + (grp + 1) * D_STATE]
            c_g = xbc_ref[:, D_SSD + (SSD_GROUPS + grp) * D_STATE:
                          D_SSD + (SSD_GROUPS + grp + 1) * D_STATE]
            cb = lax.dot_general(c_g, b_g, (((1,), (1,)), ((), ())),
                                 preferred_element_type=F32).astype(BF16)
            c_f = c_g.astype(F32)
            b_t = bt_ref[grp * D_STATE:(grp + 1) * D_STATE, :].astype(F32)
        x_f = xbc_ref[:, pair * LANES:(pair + 1) * LANES].astype(F32)
        x_pad = (jnp.where(low_half, x_f, 0.0).astype(BF16),
                 jnp.where(low_half, 0.0, x_f).astype(BF16))
        ys = []
        for j in range(2):
            hd = 2 * pair + j
            hl = lane_off + hd
            p_col = jnp.broadcast_to(pot2[:, hl:hl + 1], (q, q))
            seg = p_col - pot2_t[hd:hd + 1, :]
            m = cb * jnp.exp2(jnp.where(mask, seg, -jnp.inf)).astype(BF16)
            c_s = (c_f * jnp.exp2(p_col)).astype(BF16)
            h_h = h_ref[hd]
            lhs = jnp.concatenate([m, c_s], axis=1)
            rhs = jnp.concatenate([x_pad[j], h_h.astype(BF16)], axis=0)
            ys.append(_dot(lhs, rhs))
            b_w = (b_t * w_t[hd:hd + 1, :]).astype(BF16)
            h_ref[hd] = (h_h * decay_lanes[:, hd * PAD_HEAD:(hd + 1) * PAD_HEAD]
                         + _dot(b_w, x_pad[j]))
        y_pair = ys[0] + ys[1]
        if forward:
            y_pair = y_pair + d_ref[:, pair * LANES:(pair + 1) * LANES] * x_f
        y_ref[:, pair * LANES:(pair + 1) * LANES] = y_pair.astype(BF16)


def _ssd_kernel(xf_ref, btf_ref, potf_ref, dttf_ref, pottf_ref,
                xb_ref, btb_ref, potb_ref, dttb_ref, pottb_ref,
                d_ref, expand_ref, yf_ref, yb_ref, hf_ref, hb_ref):
    c = pl.program_id(1)

    @pl.when(c == 0)
    def _():
        hf_ref[...] = jnp.zeros_like(hf_ref)
        hb_ref[...] = jnp.zeros_like(hb_ref)

    _ssd_direction(True, 0, xf_ref, btf_ref, potf_ref, dttf_ref, pottf_ref,
                   d_ref, expand_ref, hf_ref, yf_ref)
    _ssd_direction(False, SSD_HEADS, xb_ref, btb_ref, potb_ref, dttb_ref, pottb_ref,
                   d_ref, expand_ref, hb_ref, yb_ref)


def _ssd(xbc, bt, pot, dtt, pott, d_full, expand, bsz, nc):
    t = xbc.shape[0]
    fwd = lambda b, c: (b * nc + c, 0)
    bwd = lambda b, c: (b * nc + nc - 1 - c, 0)
    specs = lambda idx: [pl.BlockSpec((CHUNK, CONV_DIM), idx),
                         pl.BlockSpec((SSD_GROUPS * D_STATE, CHUNK), idx),
                         pl.BlockSpec((CHUNK, LANES), idx),
                         pl.BlockSpec((2 * SSD_HEADS, CHUNK), idx),
                         pl.BlockSpec((2 * SSD_HEADS, CHUNK), idx)]
    y_shape = jax.ShapeDtypeStruct((t, D_SSD), BF16)
    state = pltpu.VMEM((SSD_HEADS, D_STATE, PAD_HEAD), F32)
    return pl.pallas_call(
        _ssd_kernel,
        grid=(bsz, nc),
        in_specs=specs(fwd) + specs(bwd) + [
            _const_spec((1, D_SSD)), _const_spec((LANES, 2 * SSD_HEADS * PAD_HEAD))],
        out_specs=[pl.BlockSpec((CHUNK, D_SSD), fwd), pl.BlockSpec((CHUNK, D_SSD), bwd)],
        out_shape=[y_shape, y_shape],
        scratch_shapes=[state, state],
        compiler_params=pltpu.CompilerParams(
            dimension_semantics=("parallel", "arbitrary"), vmem_limit_bytes=VMEM_LIMIT),
        name="ssd",
    )(xbc, bt, pot, dtt, pott, xbc, bt, pot, dtt, pott, d_full, expand)


def _mixout_kernel(x_ref, yf_ref, yb_ref, z_ref, u_ref, v_ref, norm_g_ref,
                   sw_ref, sb_ref, wo_ssd_ref, wo_gmlp_ref, post_g_ref, o_ref, yg_ref):
    y = yf_ref[...].astype(F32) + yb_ref[...].astype(F32)
    y_ssd = _rms(y * _silu(z_ref[...].astype(F32)), norm_g_ref[...]).astype(BF16)
    m = _dot(y_ssd, wo_ssd_ref[...])
    for ci in range(TOKEN_TILE // CHUNK):
        rows = slice(ci * CHUNK, (ci + 1) * CHUNK)
        v = v_ref[rows, :]
        mixed = jnp.concatenate(
            [_dot(sw_ref[g], v[:, g * GMLP_GROUP_DIM:(g + 1) * GMLP_GROUP_DIM])
             for g in range(GMLP_GROUPS)], axis=1)
        yg_ref[rows, :] = (u_ref[rows, :].astype(F32) * (mixed + sb_ref[...])).astype(BF16)
    m = m + _dot(yg_ref[...], wo_gmlp_ref[...])
    o_ref[...] = x_ref[...] + _rms(m, post_g_ref[...])


def _mixout(x, yf, yb, z, u, v, norm_g, sw, sb_full, wo_ssd, wo_gmlp, post_g):
    t = x.shape[0]
    row = pl.BlockSpec((TOKEN_TILE, D_MODEL), lambda i: (i, 0))
    return pl.pallas_call(
        _mixout_kernel,
        grid=(t // TOKEN_TILE,),
        in_specs=[row] * 6 + [
            _const_spec((1, D_SSD)), _const_spec((GMLP_GROUPS, CHUNK, CHUNK)),
            _const_spec((CHUNK, D_GMLP)), _const_spec((D_SSD, D_MODEL)),
            _const_spec((D_GMLP, D_MODEL)), _const_spec((1, D_MODEL))],
        out_specs=row,
        out_shape=jax.ShapeDtypeStruct((t, D_MODEL), F32),
        scratch_shapes=[pltpu.VMEM((TOKEN_TILE, D_GMLP), BF16)],
        compiler_params=pltpu.CompilerParams(
            dimension_semantics=("parallel",), vmem_limit_bytes=VMEM_LIMIT),
        name="mixout",
    )(x, yf, yb, z, u, v, norm_g, sw, sb_full, wo_ssd, wo_gmlp, post_g)


def _pad_lanes(row):
    return jnp.pad(row, ((0, 0), (0, LANES - row.shape[1])))


def kernel(x, ff1_pre_g, ff1_w_gate, ff1_w_up, ff1_w_down, ff1_post_g, mix_pre_g, w_in, conv_w, conv_b, dt_bias_f, dt_bias_b, a_log_f, a_log_b, d_skip, ssd_norm_g, gmlp_ln_g, gmlp_ln_b, spatial_w, spatial_b, w_out, mix_post_g, ff2_pre_g, ff2_w_gate, ff2_w_up, ff2_w_down, ff2_post_g):
    bsz, seqlen, _ = x.shape
    depth = w_in.shape[0]
    nc = seqlen // CHUNK
    t = bsz * seqlen
    assert seqlen % TOKEN_TILE == 0 and TOKEN_TILE % CHUNK == 0
    xt = x.reshape(t, D_MODEL)
    row = lambda p: p.reshape(1, -1)
    expand = (jnp.arange(LANES)[:, None]
              == (jnp.arange(2 * SSD_HEADS * PAD_HEAD) // PAD_HEAD)[None, :]).astype(BF16)
    o_z, o_xbc = D_SSD, D_SSD + CONV_DIM
    o_dt, o_u = o_xbc, o_xbc + 2 * SSD_HEADS
    o_v = o_u + D_GMLP
    for l in range(depth):
        xt = _ffn(xt, row(ff1_pre_g[l]), ff1_w_gate[l].astype(BF16), ff1_w_up[l].astype(BF16),
                  ff1_w_down[l].astype(BF16), row(ff1_post_g[l]))
        w = w_in[l]
        w_dt = jnp.pad(w[:, o_dt:o_u], ((0, 0), (0, LANES - 2 * SSD_HEADS))).astype(BF16)
        dt_bias = _pad_lanes(jnp.concatenate([dt_bias_f[l], dt_bias_b[l]]).reshape(1, -1))
        a_log_row = _pad_lanes(jnp.concatenate([a_log_f[l], a_log_b[l]]).reshape(1, -1))
        conv_w_pad = jnp.pad(conv_w[l], ((0, SUBLANES - CONV_WIDTH), (0, 0)))
        z, xbc, bt, pot, dtt, pott, u, v = _inproj(
            xt, row(mix_pre_g[l]), w[:, :o_z].astype(BF16), w[:, o_z:o_xbc].astype(BF16), w_dt,
            w[:, o_u:o_v].astype(BF16), w[:, o_v:].astype(BF16), dt_bias, a_log_row,
            row(gmlp_ln_g[l]), row(gmlp_ln_b[l]), conv_w_pad, row(conv_b[l]), seqlen)
        d_full = jnp.repeat(d_skip[l], SSD_HEAD_DIM).reshape(1, D_SSD)
        yf, yb = _ssd(xbc, bt, pot, dtt, pott, d_full, expand, bsz, nc)
        sb_full = jnp.repeat(spatial_b[l].T, GMLP_GROUP_DIM, axis=1)
        xt = _mixout(xt, yf, yb, z, u, v, row(ssd_norm_g[l]), spatial_w[l].astype(BF16), sb_full,
                     w_out[l, :D_SSD].astype(BF16), w_out[l, D_SSD:].astype(BF16), row(mix_post_g[l]))
        xt = _ffn(xt, row(ff2_pre_g[l]), ff2_w_gate[l].astype(BF16), ff2_w_up[l].astype(BF16),
                  ff2_w_down[l].astype(BF16), row(ff2_post_g[l]))
    return xt.reshape(bsz, seqlen, D_MODEL)
```

```python
import functools
import math

import jax
import jax.numpy as jnp
from jax import lax
from jax.experimental import pallas as pl
from jax.experimental.pallas import tpu as pltpu

F32 = jnp.float32
BF16 = jnp.bfloat16

D_MODEL = 1024
D_SSD = 1024
D_GMLP = 1024
SSD_HEAD_DIM = 64
SSD_HEADS = D_SSD // SSD_HEAD_DIM
SSD_GROUPS = 4
HEADS_PER_GROUP = SSD_HEADS // SSD_GROUPS
D_STATE = 128
CONV_WIDTH = 5
CONV_PAD = (CONV_WIDTH - 1) // 2
CONV_DIM = D_SSD + 2 * SSD_GROUPS * D_STATE
CHUNK = 128
GMLP_GROUPS = 8
GMLP_GROUP_DIM = D_GMLP // GMLP_GROUPS
D_FF = 2816
EPS = 1e-6
LOG2E = math.log2(math.e)

LANES = 128
SUBLANES = 8
PAD_HEAD = LANES
HALO = 16
TOKEN_TILE = 512
FF_TILE = 512
CONV_TILE = 512
VMEM_LIMIT = 56 * 1024 * 1024


def _rms(x, g):
    return x * lax.rsqrt(jnp.mean(x * x, axis=-1, keepdims=True) + EPS) * g


def _silu(x):
    return x * jax.nn.sigmoid(x)


def _gelu(x):
    return 0.5 * x * (1.0 + lax.erf(x * math.sqrt(0.5)))


def _softplus(x):
    return jnp.maximum(x, 0.0) + jnp.log1p(jnp.exp(-jnp.abs(x)))


def _split3(v):
    hi = v.astype(BF16)
    r1 = v - hi.astype(F32)
    mid = r1.astype(BF16)
    lo = (r1 - mid.astype(F32)).astype(BF16)
    return hi, mid, lo


def _dot(a, b):
    return jnp.dot(a, b, preferred_element_type=F32)


def _dot01_left(m01, v):
    hi, mid, lo = _split3(v)
    return _dot(m01, hi) + _dot(m01, mid) + _dot(m01, lo)


def _tri_mask(n, lower):
    rows = lax.broadcasted_iota(jnp.int32, (n, n), 0)
    cols = lax.broadcasted_iota(jnp.int32, (n, n), 1)
    return cols <= rows if lower else cols >= rows


def _const_spec(shape):
    zeros = (0,) * len(shape)
    return pl.BlockSpec(shape, lambda *_: zeros, pipeline_mode=pl.Buffered(1))


def _ffn_kernel(x_ref, pre_g_ref, wg_ref, wu_ref, wd_ref, post_g_ref, o_ref):
    x = x_ref[...]
    h = _rms(x, pre_g_ref[...]).astype(BF16)
    acc = jnp.zeros(x.shape, F32)
    for start in range(0, D_FF, FF_TILE):
        size = min(FF_TILE, D_FF - start)
        g = _dot(h, wg_ref[:, start:start + size])
        u = _dot(h, wu_ref[:, start:start + size])
        a = (_silu(g) * u).astype(BF16)
        acc = acc + _dot(a, wd_ref[start:start + size, :])
    o_ref[...] = x + 0.5 * _rms(acc, post_g_ref[...])


def _ffn(x, pre_g, wg, wu, wd, post_g):
    t = x.shape[0]
    row = pl.BlockSpec((TOKEN_TILE, D_MODEL), lambda i: (i, 0))
    return pl.pallas_call(
        _ffn_kernel,
        grid=(t // TOKEN_TILE,),
        in_specs=[row, _const_spec((1, D_MODEL)), _const_spec((D_MODEL, D_FF)),
                  _const_spec((D_MODEL, D_FF)), _const_spec((D_FF, D_MODEL)),
                  _const_spec((1, D_MODEL))],
        out_specs=row,
        out_shape=jax.ShapeDtypeStruct((t, D_MODEL), F32),
        compiler_params=pltpu.CompilerParams(
            dimension_semantics=("parallel",), vmem_limit_bytes=VMEM_LIMIT),
        name="ffn",
    )(x, pre_g, wg, wu, wd, post_g)


def _inproj_kernel(x_ref, x_prev_ref, x_next_ref, g_ref, wz_ref, wxbc_ref, wdt_ref, wu_ref,
                   wv_ref, dt_bias_ref, a_log_ref, ln_g_ref, ln_b_ref, conv_w_ref, conv_b_ref,
                   z_ref, xbc_ref, bt_ref, pot_ref, dtt_ref, pott_ref, u_ref, v_ref,
                   ext_ref, *, tiles_per_seq):
    i = pl.program_id(0)
    first = (i % tiles_per_seq) == 0
    last = (i % tiles_per_seq) == tiles_per_seq - 1
    g = g_ref[...]
    h = _rms(x_ref[...], g).astype(BF16)
    h_ext = jnp.concatenate([_rms(x_prev_ref[...], g).astype(BF16), h,
                             _rms(x_next_ref[...], g).astype(BF16)], axis=0)
    n = TOKEN_TILE
    chunks = TOKEN_TILE // CHUNK
    b_cols = D_SSD
    for start in range(0, CONV_DIM, CONV_TILE):
        cols = slice(start, start + CONV_TILE)
        xe = _dot(h_ext, wxbc_ref[:, cols])
        ext_ref[0:HALO, :] = jnp.where(first, 0.0, xe[0:HALO])
        ext_ref[HALO:HALO + n, :] = xe[HALO:HALO + n]
        ext_ref[HALO + n:, :] = jnp.where(last, 0.0, xe[HALO + n:])
        conv = conv_b_ref[:, cols] + conv_w_ref[0:1, cols] * ext_ref[HALO - CONV_PAD:HALO - CONV_PAD + n, :]
        for k in range(1, CONV_WIDTH):
            off = HALO - CONV_PAD + k
            conv = conv + conv_w_ref[k:k + 1, cols] * ext_ref[off:off + n, :]
        act = _silu(conv)
        xbc_ref[:, cols] = act.astype(BF16)
        if start == b_cols:
            for c in range(chunks):
                for grp in range(SSD_GROUPS):
                    blk = act[c * CHUNK:(c + 1) * CHUNK, grp * D_STATE:(grp + 1) * D_STATE]
                    r0 = (c * SSD_GROUPS + grp) * D_STATE
                    bt_ref[r0:r0 + D_STATE, :] = blk.T.astype(BF16)

    z_ref[...] = _dot(h, wz_ref[...]).astype(BF16)
    dt = _softplus(_dot(h, wdt_ref[...]) + dt_bias_ref[...])
    head_lane = lax.broadcasted_iota(jnp.int32, (1, LANES), 1)
    a_row = jnp.where(head_lane < 2 * SSD_HEADS, -jnp.exp(a_log_ref[...]), 0.0)
    tril01 = jnp.where(_tri_mask(CHUNK, True), 1.0, 0.0).astype(BF16)
    for c in range(chunks):
        rows = slice(c * CHUNK, (c + 1) * CHUNK)
        dta = dt[rows, :] * a_row
        incl = _dot01_left(tril01, dta)
        pot = jnp.where(head_lane < SSD_HEADS, incl, incl[CHUNK - 1:CHUNK, :] + dta - incl)
        pot_ref[rows, :] = pot
        hrows = slice(c * 2 * SSD_HEADS, (c + 1) * 2 * SSD_HEADS)
        pott_ref[hrows, :] = pot.T[0:2 * SSD_HEADS, :]
        dtt_ref[hrows, :] = dt[rows, :].T[0:2 * SSD_HEADS, :]
    u_ref[...] = _gelu(_dot(h, wu_ref[...])).astype(BF16)
    v = _gelu(_dot(h, wv_ref[...]))
    vc = v - jnp.mean(v, axis=-1, keepdims=True)
    vn = vc * lax.rsqrt(jnp.mean(vc * vc, axis=-1, keepdims=True) + EPS)
    v_ref[...] = (vn * ln_g_ref[...] + ln_b_ref[...]).astype(BF16)


def _inproj(x, g, wz, wxbc, wdt, wu, wv, dt_bias, a_log_row, ln_g, ln_b, conv_w, conv_b, seqlen):
    t = x.shape[0]
    chunks = TOKEN_TILE // CHUNK
    halo_per_tile = TOKEN_TILE // HALO
    last_halo = t // HALO - 1
    row = lambda w, n=TOKEN_TILE: pl.BlockSpec((n, w), lambda i: (i, 0))
    prev = pl.BlockSpec((HALO, D_MODEL), lambda i: (jnp.maximum(i * halo_per_tile - 1, 0), 0))
    nxt = pl.BlockSpec((HALO, D_MODEL),
                       lambda i: (jnp.minimum((i + 1) * halo_per_tile, last_halo), 0))
    out = lambda w, dt, n=t: jax.ShapeDtypeStruct((n, w), dt)
    bt_rows = SSD_GROUPS * D_STATE
    hd_rows = 2 * SSD_HEADS
    return pl.pallas_call(
        functools.partial(_inproj_kernel, tiles_per_seq=seqlen // TOKEN_TILE),
        grid=(t // TOKEN_TILE,),
        in_specs=[row(D_MODEL), prev, nxt, _const_spec((1, D_MODEL)),
                  _const_spec((D_MODEL, D_SSD)), _const_spec((D_MODEL, CONV_DIM)),
                  _const_spec((D_MODEL, LANES)), _const_spec((D_MODEL, D_GMLP)),
                  _const_spec((D_MODEL, D_GMLP)), _const_spec((1, LANES)),
                  _const_spec((1, LANES)), _const_spec((1, D_GMLP)), _const_spec((1, D_GMLP)),
                  _const_spec((SUBLANES, CONV_DIM)), _const_spec((1, CONV_DIM))],
        out_specs=[row(D_SSD), row(CONV_DIM), row(CHUNK, chunks * bt_rows), row(LANES),
                   row(CHUNK, chunks * hd_rows), row(CHUNK, chunks * hd_rows),
                   row(D_GMLP), row(D_GMLP)],
        out_shape=[out(D_SSD, BF16), out(CONV_DIM, BF16),
                   out(CHUNK, BF16, t // CHUNK * bt_rows), out(LANES, F32),
                   out(CHUNK, F32, t // CHUNK * hd_rows), out(CHUNK, F32, t // CHUNK * hd_rows),
                   out(D_GMLP, BF16), out(D_GMLP, BF16)],
        scratch_shapes=[pltpu.VMEM((TOKEN_TILE + 2 * HALO, CONV_TILE), F32)],
        compiler_params=pltpu.CompilerParams(
            dimension_semantics=("parallel",), vmem_limit_bytes=VMEM_LIMIT),
        name="inproj",
    )(x, x, x, g, wz, wxbc, wdt, wu, wv, dt_bias, a_log_row, ln_g, ln_b, conv_w, conv_b)


def _ssd_direction(forward, lane_off, xbc_ref, bt_ref, pot_ref, dtt_ref, pott_ref,
                   d_ref, h_ref, y_ref):
    q = CHUNK
    mask = _tri_mask(q, forward)
    pot = pot_ref[...]
    dt_t = dtt_ref[lane_off:lane_off + SSD_HEADS, :]
    pot_t = pott_ref[lane_off:lane_off + SSD_HEADS, :]
    far = q - 1 if forward else 0
    total = pot[far:far + 1, :]
    w_t = jnp.exp(pot_t[:, far:far + 1] - pot_t) * dt_t
    pot2 = pot * LOG2E
    pot2_t = (pot_t - jnp.log(dt_t)) * LOG2E

    low_half = lax.broadcasted_iota(jnp.int32, (1, LANES), 1) < SSD_HEAD_DIM
    for pair in range(SSD_HEADS // 2):
        grp = pair // (HEADS_PER_GROUP // 2)
        if pair % (HEADS_PER_GROUP // 2) == 0:
            b_g = xbc_ref[:, D_SSD + grp * D_STATE:D_SSD + (grp + 1) * D_STATE]
            c_g = xbc_ref[:, D_SSD + (SSD_GROUPS + grp) * D_STATE:
                          D_SSD + (SSD_GROUPS + grp + 1) * D_STATE]
            cb = lax.dot_general(c_g, b_g, (((1,), (1,)), ((), ())),
                                 preferred_element_type=F32).astype(BF16)
            c_f = c_g.astype(F32)
            b_t = bt_ref[grp * D_STATE:(grp + 1) * D_STATE, :].astype(F32)
        x_f = xbc_ref[:, pair * LANES:(pair + 1) * LANES].astype(F32)
        x_pad = (jnp.where(low_half, x_f, 0.0).astype(BF16),
                 jnp.where(low_half, 0.0, x_f).astype(BF16))
        ys = []
        for j in range(2):
            hd = 2 * pair + j
            hl = lane_off + hd
            p_col = jnp.broadcast_to(pot2[:, hl:hl + 1], (q, q))
            seg = p_col - pot2_t[hd:hd + 1, :]
            m = cb * jnp.exp2(jnp.where(mask, seg, -jnp.inf)).astype(BF16)
            c_s = (c_f * jnp.exp2(p_col)).astype(BF16)
            h_h = h_ref[hd]
            lhs = jnp.concatenate([m, c_s], axis=1)
            rhs = jnp.concatenate([x_pad[j], h_h.astype(BF16)], axis=0)
            ys.append(_dot(lhs, rhs))
            b_w = (b_t * w_t[hd:hd + 1, :]).astype(BF16)
            decay = jnp.exp(jnp.broadcast_to(total[:, hl:hl + 1], (1, PAD_HEAD)))
            h_ref[hd] = h_h * decay + _dot(b_w, x_pad[j])
        y_pair = ys[0] + ys[1]
        if forward:
            y_pair = y_pair + d_ref[:, pair * LANES:(pair + 1) * LANES] * x_f
        y_ref[:, pair * LANES:(pair + 1) * LANES] = y_pair.astype(BF16)


def _ssd_kernel(xf_ref, btf_ref, potf_ref, dttf_ref, pottf_ref,
                xb_ref, btb_ref, potb_ref, dttb_ref, pottb_ref,
                d_ref, yf_ref, yb_ref, hf_ref, hb_ref):
    c = pl.program_id(1)

    @pl.when(c == 0)
    def _():
        hf_ref[...] = jnp.zeros_like(hf_ref)
        hb_ref[...] = jnp.zeros_like(hb_ref)

    _ssd_direction(True, 0, xf_ref, btf_ref, potf_ref, dttf_ref, pottf_ref,
                   d_ref, hf_ref, yf_ref)
    _ssd_direction(False, SSD_HEADS, xb_ref, btb_ref, potb_ref, dttb_ref, pottb_ref,
                   d_ref, hb_ref, yb_ref)


def _ssd(xbc, bt, pot, dtt, pott, d_full, bsz, nc):
    t = xbc.shape[0]
    fwd = lambda b, c: (b * nc + c, 0)
    bwd = lambda b, c: (b * nc + nc - 1 - c, 0)
    specs = lambda idx: [pl.BlockSpec((CHUNK, CONV_DIM), idx),
                         pl.BlockSpec((SSD_GROUPS * D_STATE, CHUNK), idx),
                         pl.BlockSpec((CHUNK, LANES), idx),
                         pl.BlockSpec((2 * SSD_HEADS, CHUNK), idx),
                         pl.BlockSpec((2 * SSD_HEADS, CHUNK), idx)]
    y_shape = jax.ShapeDtypeStruct((t, D_SSD), BF16)
    state = pltpu.VMEM((SSD_HEADS, D_STATE, PAD_HEAD), F32)
    return pl.pallas_call(
        _ssd_kernel,
        grid=(bsz, nc),
        in_specs=specs(fwd) + specs(bwd) + [
            _const_spec((1, D_SSD))],
        out_specs=[pl.BlockSpec((CHUNK, D_SSD), fwd), pl.BlockSpec((CHUNK, D_SSD), bwd)],
        out_shape=[y_shape, y_shape],
        scratch_shapes=[state, state],
        compiler_params=pltpu.CompilerParams(
            dimension_semantics=("parallel", "arbitrary"), vmem_limit_bytes=VMEM_LIMIT),
        name="ssd",
    )(xbc, bt, pot, dtt, pott, xbc, bt, pot, dtt, pott, d_full)


def _mixout_kernel(x_ref, yf_ref, yb_ref, z_ref, u_ref, v_ref, norm_g_ref,
                   sw_ref, sb_ref, wo_ssd_ref, wo_gmlp_ref, post_g_ref, o_ref, yg_ref):
    y = yf_ref[...].astype(F32) + yb_ref[...].astype(F32)
    y_ssd = _rms(y * _silu(z_ref[...].astype(F32)), norm_g_ref[...]).astype(BF16)
    m = _dot(y_ssd, wo_ssd_ref[...])
    for ci in range(TOKEN_TILE // CHUNK):
        rows = slice(ci * CHUNK, (ci + 1) * CHUNK)
        v = v_ref[rows, :]
        mixed = jnp.concatenate(
            [_dot(sw_ref[g], v[:, g * GMLP_GROUP_DIM:(g + 1) * GMLP_GROUP_DIM])
             for g in range(GMLP_GROUPS)], axis=1)
        yg_ref[rows, :] = (u_ref[rows, :].astype(F32) * (mixed + sb_ref[...])).astype(BF16)
    m = m + _dot(yg_ref[...], wo_gmlp_ref[...])
    o_ref[...] = x_ref[...] + _rms(m, post_g_ref[...])


def _mixout(x, yf, yb, z, u, v, norm_g, sw, sb_full, wo_ssd, wo_gmlp, post_g):
    t = x.shape[0]
    row = pl.BlockSpec((TOKEN_TILE, D_MODEL), lambda i: (i, 0))
    return pl.pallas_call(
        _mixout_kernel,
        grid=(t // TOKEN_TILE,),
        in_specs=[row] * 6 + [
            _const_spec((1, D_SSD)), _const_spec((GMLP_GROUPS, CHUNK, CHUNK)),
            _const_spec((CHUNK, D_GMLP)), _const_spec((D_SSD, D_MODEL)),
            _const_spec((D_GMLP, D_MODEL)), _const_spec((1, D_MODEL))],
        out_specs=row,
        out_shape=jax.ShapeDtypeStruct((t, D_MODEL), F32),
        scratch_shapes=[pltpu.VMEM((TOKEN_TILE, D_GMLP), BF16)],
        compiler_params=pltpu.CompilerParams(
            dimension_semantics=("parallel",), vmem_limit_bytes=VMEM_LIMIT),
        name="mixout",
    )(x, yf, yb, z, u, v, norm_g, sw, sb_full, wo_ssd, wo_gmlp, post_g)


def _pad_lanes(row):
    return jnp.pad(row, ((0, 0), (0, LANES - row.shape[1])))


def kernel(x, ff1_pre_g, ff1_w_gate, ff1_w_up, ff1_w_down, ff1_post_g, mix_pre_g, w_in, conv_w, conv_b, dt_bias_f, dt_bias_b, a_log_f, a_log_b, d_skip, ssd_norm_g, gmlp_ln_g, gmlp_ln_b, spatial_w, spatial_b, w_out, mix_post_g, ff2_pre_g, ff2_w_gate, ff2_w_up, ff2_w_down, ff2_post_g):
    bsz, seqlen, _ = x.shape
    depth = w_in.shape[0]
    nc = seqlen // CHUNK
    t = bsz * seqlen
    assert seqlen % TOKEN_TILE == 0 and TOKEN_TILE % CHUNK == 0
    xt = x.reshape(t, D_MODEL)
    row = lambda p: p.reshape(1, -1)
    o_z, o_xbc = D_SSD, D_SSD + CONV_DIM
    o_dt, o_u = o_xbc, o_xbc + 2 * SSD_HEADS
    o_v = o_u + D_GMLP
    for l in range(depth):
        xt = _ffn(xt, row(ff1_pre_g[l]), ff1_w_gate[l].astype(BF16), ff1_w_up[l].astype(BF16),
                  ff1_w_down[l].astype(BF16), row(ff1_post_g[l]))
        w = w_in[l]
        w_dt = jnp.pad(w[:, o_dt:o_u], ((0, 0), (0, LANES - 2 * SSD_HEADS))).astype(BF16)
        dt_bias = _pad_lanes(jnp.concatenate([dt_bias_f[l], dt_bias_b[l]]).reshape(1, -1))
        a_log_row = _pad_lanes(jnp.concatenate([a_log_f[l], a_log_b[l]]).reshape(1, -1))
        conv_w_pad = jnp.pad(conv_w[l], ((0, SUBLANES - CONV_WIDTH), (0, 0)))
        z, xbc, bt, pot, dtt, pott, u, v = _inproj(
            xt, row(mix_pre_g[l]), w[:, :o_z].astype(BF16), w[:, o_z:o_xbc].astype(BF16), w_dt,
            w[:, o_u:o_v].astype(BF16), w[:, o_v:].astype(BF16), dt_bias, a_log_row,
            row(gmlp_ln_g[l]), row(gmlp_ln_b[l]), conv_w_pad, row(conv_b[l]), seqlen)
        d_full = jnp.repeat(d_skip[l], SSD_HEAD_DIM).reshape(1, D_SSD)
        yf, yb = _ssd(xbc, bt, pot, dtt, pott, d_full, bsz, nc)
        sb_full = jnp.repeat(spatial_b[l].T, GMLP_GROUP_DIM, axis=1)
        xt = _mixout(xt, yf, yb, z, u, v, row(ssd_norm_g[l]), spatial_w[l].astype(BF16), sb_full,
                     w_out[l, :D_SSD].astype(BF16), w_out[l, D_SSD:].astype(BF16), row(mix_post_g[l]))
        xt = _ffn(xt, row(ff2_pre_g[l]), ff2_w_gate[l].astype(BF16), ff2_w_up[l].astype(BF16),
                  ff2_w_down[l].astype(BF16), row(ff2_post_g[l]))
    return xt.reshape(bsz, seqlen, D_MODEL)
```

```python
import functools
import math

import jax
import jax.numpy as jnp
from jax import lax
from jax.experimental import pallas as pl
from jax.experimental.pallas import tpu as pltpu

F32 = jnp.float32
BF16 = jnp.bfloat16

D_MODEL = 1024
D_SSD = 1024
D_GMLP = 1024
SSD_HEAD_DIM = 64
SSD_HEADS = D_SSD // SSD_HEAD_DIM
SSD_GROUPS = 4
HEADS_PER_GROUP = SSD_HEADS // SSD_GROUPS
D_STATE = 128
CONV_WIDTH = 5
CONV_PAD = (CONV_WIDTH - 1) // 2
CONV_DIM = D_SSD + 2 * SSD_GROUPS * D_STATE
CHUNK = 128
GMLP_GROUPS = 8
GMLP_GROUP_DIM = D_GMLP // GMLP_GROUPS
D_FF = 2816
EPS = 1e-6
LOG2E = math.log2(math.e)

LANES = 128
SUBLANES = 8
PAD_HEAD = LANES
HALO = 16
TOKEN_TILE = 512
FF_TILE = 512
CONV_TILE = 512
VMEM_LIMIT = 56 * 1024 * 1024


def _rms(x, g):
    return x * lax.rsqrt(jnp.mean(x * x, axis=-1, keepdims=True) + EPS) * g


def _silu(x):
    return x * jax.nn.sigmoid(x)


def _gelu(x):
    return 0.5 * x * (1.0 + lax.erf(x * math.sqrt(0.5)))


def _softplus(x):
    return jnp.maximum(x, 0.0) + jnp.log1p(jnp.exp(-jnp.abs(x)))


def _split3(v):
    hi = v.astype(BF16)
    r1 = v - hi.astype(F32)
    mid = r1.astype(BF16)
    lo = (r1 - mid.astype(F32)).astype(BF16)
    return hi, mid, lo


def _dot(a, b):
    return jnp.dot(a, b, preferred_element_type=F32)


def _dot01_left(m01, v):
    hi, mid, lo = _split3(v)
    return _dot(m01, hi) + _dot(m01, mid) + _dot(m01, lo)


def _tri_mask(n, lower):
    rows = lax.broadcasted_iota(jnp.int32, (n, n), 0)
    cols = lax.broadcasted_iota(jnp.int32, (n, n), 1)
    return cols <= rows if lower else cols >= rows


def _const_spec(shape):
    zeros = (0,) * len(shape)
    return pl.BlockSpec(shape, lambda *_: zeros, pipeline_mode=pl.Buffered(1))


def _ffn_kernel(x_ref, pre_g_ref, wg_ref, wu_ref, wd_ref, post_g_ref, o_ref):
    x = x_ref[...]
    h = _rms(x, pre_g_ref[...]).astype(BF16)
    acc = jnp.zeros(x.shape, F32)
    for start in range(0, D_FF, FF_TILE):
        size = min(FF_TILE, D_FF - start)
        g = _dot(h, wg_ref[:, start:start + size])
        u = _dot(h, wu_ref[:, start:start + size])
        a = (_silu(g) * u).astype(BF16)
        acc = acc + _dot(a, wd_ref[start:start + size, :])
    o_ref[...] = x + 0.5 * _rms(acc, post_g_ref[...])


def _ffn(x, pre_g, wg, wu, wd, post_g):
    t = x.shape[0]
    row = pl.BlockSpec((TOKEN_TILE, D_MODEL), lambda i: (i, 0))
    return pl.pallas_call(
        _ffn_kernel,
        grid=(t // TOKEN_TILE,),
        in_specs=[row, _const_spec((1, D_MODEL)), _const_spec((D_MODEL, D_FF)),
                  _const_spec((D_MODEL, D_FF)), _const_spec((D_FF, D_MODEL)),
                  _const_spec((1, D_MODEL))],
        out_specs=row,
        out_shape=jax.ShapeDtypeStruct((t, D_MODEL), F32),
        compiler_params=pltpu.CompilerParams(
            dimension_semantics=("parallel",), vmem_limit_bytes=VMEM_LIMIT),
        name="ffn",
    )(x, pre_g, wg, wu, wd, post_g)


def _inproj_kernel(x_ref, x_prev_ref, x_next_ref, g_ref, wz_ref, wxbc_ref, wdt_ref, wu_ref,
                   wv_ref, dt_bias_ref, a_log_ref, ln_g_ref, ln_b_ref, conv_w_ref, conv_b_ref,
                   z_ref, xbc_ref, bt_ref, pot_ref, dtt_ref, pott_ref, u_ref, v_ref,
                   ext_ref, act_ref, *, tiles_per_seq):
    i = pl.program_id(0)
    first = (i % tiles_per_seq) == 0
    last = (i % tiles_per_seq) == tiles_per_seq - 1
    g = g_ref[...]
    h = _rms(x_ref[...], g).astype(BF16)
    h_ext = jnp.concatenate([_rms(x_prev_ref[...], g).astype(BF16), h,
                             _rms(x_next_ref[...], g).astype(BF16)], axis=0)
    n = TOKEN_TILE
    half = n // 2
    chunks = TOKEN_TILE // CHUNK
    slabs = CONV_TILE // LANES
    b_slab0 = D_SSD // LANES
    for start in range(0, CONV_DIM, CONV_TILE):
        xe = _dot(h_ext, wxbc_ref[:, start:start + CONV_TILE])
        for sl in range(slabs):
            s_idx = start // LANES + sl
            lanes = slice(sl * LANES, (sl + 1) * LANES)
            ext_ref[s_idx, 0:HALO, :] = jnp.where(first, 0.0, xe[0:HALO, lanes])
            ext_ref[s_idx, HALO:HALO + n, :] = xe[HALO:HALO + n, lanes]
            ext_ref[s_idx, HALO + n:, :] = jnp.where(last, 0.0, xe[HALO + n:, lanes])
        for sl in range(slabs):
            s_idx = start // LANES + sl
            cols = slice(s_idx * LANES, (s_idx + 1) * LANES)
            for parity in range(2):
                acc = conv_b_ref[:, cols]
                for k in range(CONV_WIDTH):
                    tap = ext_ref[s_idx, pl.ds(HALO - CONV_PAD + k + parity, half, stride=2), :]
                    acc = acc + conv_w_ref[k:k + 1, cols] * tap
                act_ref[s_idx, pl.ds(parity, half, stride=2), :] = _silu(acc)
            act = act_ref[s_idx]
            xbc_ref[:, cols] = act.astype(BF16)
            if b_slab0 <= s_idx < b_slab0 + SSD_GROUPS:
                grp = s_idx - b_slab0
                for c in range(chunks):
                    r0 = (c * SSD_GROUPS + grp) * D_STATE
                    bt_ref[r0:r0 + D_STATE, :] = act[c * CHUNK:(c + 1) * CHUNK, :].T.astype(BF16)

    z_ref[...] = _dot(h, wz_ref[...]).astype(BF16)
    dt = _softplus(_dot(h, wdt_ref[...]) + dt_bias_ref[...])
    head_lane = lax.broadcasted_iota(jnp.int32, (1, LANES), 1)
    a_row = jnp.where(head_lane < 2 * SSD_HEADS, -jnp.exp(a_log_ref[...]), 0.0)
    tril01 = jnp.where(_tri_mask(CHUNK, True), 1.0, 0.0).astype(BF16)
    for c in range(chunks):
        rows = slice(c * CHUNK, (c + 1) * CHUNK)
        dta = dt[rows, :] * a_row
        incl = _dot01_left(tril01, dta)
        pot = jnp.where(head_lane < SSD_HEADS, incl, incl[CHUNK - 1:CHUNK, :] + dta - incl)
        pot_ref[rows, :] = pot
        hrows = slice(c * 2 * SSD_HEADS, (c + 1) * 2 * SSD_HEADS)
        pott_ref[hrows, :] = pot.T[0:2 * SSD_HEADS, :]
        dtt_ref[hrows, :] = dt[rows, :].T[0:2 * SSD_HEADS, :]
    u_ref[...] = _gelu(_dot(h, wu_ref[...])).astype(BF16)
    v = _gelu(_dot(h, wv_ref[...]))
    vc = v - jnp.mean(v, axis=-1, keepdims=True)
    vn = vc * lax.rsqrt(jnp.mean(vc * vc, axis=-1, keepdims=True) + EPS)
    v_ref[...] = (vn * ln_g_ref[...] + ln_b_ref[...]).astype(BF16)


def _inproj(x, g, wz, wxbc, wdt, wu, wv, dt_bias, a_log_row, ln_g, ln_b, conv_w, conv_b, seqlen):
    t = x.shape[0]
    chunks = TOKEN_TILE // CHUNK
    halo_per_tile = TOKEN_TILE // HALO
    last_halo = t // HALO - 1
    row = lambda w, n=TOKEN_TILE: pl.BlockSpec((n, w), lambda i: (i, 0))
    prev = pl.BlockSpec((HALO, D_MODEL), lambda i: (jnp.maximum(i * halo_per_tile - 1, 0), 0))
    nxt = pl.BlockSpec((HALO, D_MODEL),
                       lambda i: (jnp.minimum((i + 1) * halo_per_tile, last_halo), 0))
    out = lambda w, dt, n=t: jax.ShapeDtypeStruct((n, w), dt)
    bt_rows = SSD_GROUPS * D_STATE
    hd_rows = 2 * SSD_HEADS
    return pl.pallas_call(
        functools.partial(_inproj_kernel, tiles_per_seq=seqlen // TOKEN_TILE),
        grid=(t // TOKEN_TILE,),
        in_specs=[row(D_MODEL), prev, nxt, _const_spec((1, D_MODEL)),
                  _const_spec((D_MODEL, D_SSD)), _const_spec((D_MODEL, CONV_DIM)),
                  _const_spec((D_MODEL, LANES)), _const_spec((D_MODEL, D_GMLP)),
                  _const_spec((D_MODEL, D_GMLP)), _const_spec((1, LANES)),
                  _const_spec((1, LANES)), _const_spec((1, D_GMLP)), _const_spec((1, D_GMLP)),
                  _const_spec((SUBLANES, CONV_DIM)), _const_spec((1, CONV_DIM))],
        out_specs=[row(D_SSD), row(CONV_DIM), row(CHUNK, chunks * bt_rows), row(LANES),
                   row(CHUNK, chunks * hd_rows), row(CHUNK, chunks * hd_rows),
                   row(D_GMLP), row(D_GMLP)],
        out_shape=[out(D_SSD, BF16), out(CONV_DIM, BF16),
                   out(CHUNK, BF16, t // CHUNK * bt_rows), out(LANES, F32),
                   out(CHUNK, F32, t // CHUNK * hd_rows), out(CHUNK, F32, t // CHUNK * hd_rows),
                   out(D_GMLP, BF16), out(D_GMLP, BF16)],
        scratch_shapes=[pltpu.VMEM((CONV_DIM // LANES, TOKEN_TILE + 2 * HALO, LANES), F32),
                        pltpu.VMEM((CONV_DIM // LANES, TOKEN_TILE, LANES), F32)],
        compiler_params=pltpu.CompilerParams(
            dimension_semantics=("parallel",), vmem_limit_bytes=VMEM_LIMIT),
        name="inproj",
    )(x, x, x, g, wz, wxbc, wdt, wu, wv, dt_bias, a_log_row, ln_g, ln_b, conv_w, conv_b)


def _ssd_direction(forward, lane_off, xbc_ref, bt_ref, pot_ref, dtt_ref, pott_ref,
                   d_ref, h_ref, y_ref):
    q = CHUNK
    mask = _tri_mask(q, forward)
    pot = pot_ref[...]
    dt_t = dtt_ref[lane_off:lane_off + SSD_HEADS, :]
    pot_t = pott_ref[lane_off:lane_off + SSD_HEADS, :]
    far = q - 1 if forward else 0
    total = pot[far:far + 1, :]
    w_t = jnp.exp(pot_t[:, far:far + 1] - pot_t) * dt_t
    pot2 = pot * LOG2E
    pot2_t = (pot_t - jnp.log(dt_t)) * LOG2E

    low_half = lax.broadcasted_iota(jnp.int32, (1, LANES), 1) < SSD_HEAD_DIM
    for pair in range(SSD_HEADS // 2):
        grp = pair // (HEADS_PER_GROUP // 2)
        if pair % (HEADS_PER_GROUP // 2) == 0:
            b_g = xbc_ref[:, D_SSD + grp * D_STATE:D_SSD + (grp + 1) * D_STATE]
            c_g = xbc_ref[:, D_SSD + (SSD_GROUPS + grp) * D_STATE:
                          D_SSD + (SSD_GROUPS + grp + 1) * D_STATE]
            cb = lax.dot_general(c_g, b_g, (((1,), (1,)), ((), ())),
                                 preferred_element_type=F32).astype(BF16)
            c_f = c_g.astype(F32)
            b_t = bt_ref[grp * D_STATE:(grp + 1) * D_STATE, :].astype(F32)
        x_f = xbc_ref[:, pair * LANES:(pair + 1) * LANES].astype(F32)
        x_pad = (jnp.where(low_half, x_f, 0.0).astype(BF16),
                 jnp.where(low_half, 0.0, x_f).astype(BF16))
        ys = []
        for j in range(2):
            hd = 2 * pair + j
            hl = lane_off + hd
            p_col = jnp.broadcast_to(pot2[:, hl:hl + 1], (q, q))
            seg = p_col - pot2_t[hd:hd + 1, :]
            m = cb * jnp.exp2(jnp.where(mask, seg, -jnp.inf)).astype(BF16)
            c_s = (c_f * jnp.exp2(p_col)).astype(BF16)
            h_h = h_ref[hd]
            lhs = jnp.concatenate([m, c_s], axis=1)
            rhs = jnp.concatenate([x_pad[j], h_h.astype(BF16)], axis=0)
            ys.append(_dot(lhs, rhs))
            b_w = (b_t * w_t[hd:hd + 1, :]).astype(BF16)
            decay = jnp.exp(jnp.broadcast_to(total[:, hl:hl + 1], (1, PAD_HEAD)))
            h_ref[hd] = h_h * decay + _dot(b_w, x_pad[j])
        y_pair = ys[0] + ys[1]
        if forward:
            y_pair = y_pair + d_ref[:, pair * LANES:(pair + 1) * LANES] * x_f
        y_ref[:, pair * LANES:(pair + 1) * LANES] = y_pair.astype(BF16)


def _ssd_kernel(xf_ref, btf_ref, potf_ref, dttf_ref, pottf_ref,
                xb_ref, btb_ref, potb_ref, dttb_ref, pottb_ref,
                d_ref, yf_ref, yb_ref, hf_ref, hb_ref):
    c = pl.program_id(1)

    @pl.when(c == 0)
    def _():
        hf_ref[...] = jnp.zeros_like(hf_ref)
        hb_ref[...] = jnp.zeros_like(hb_ref)

    _ssd_direction(True, 0, xf_ref, btf_ref, potf_ref, dttf_ref, pottf_ref,
                   d_ref, hf_ref, yf_ref)
    _ssd_direction(False, SSD_HEADS, xb_ref, btb_ref, potb_ref, dttb_ref, pottb_ref,
                   d_ref, hb_ref, yb_ref)


def _ssd(xbc, bt, pot, dtt, pott, d_full, bsz, nc):
    t = xbc.shape[0]
    fwd = lambda b, c: (b * nc + c, 0)
    bwd = lambda b, c: (b * nc + nc - 1 - c, 0)
    specs = lambda idx: [pl.BlockSpec((CHUNK, CONV_DIM), idx),
                         pl.BlockSpec((SSD_GROUPS * D_STATE, CHUNK), idx),
                         pl.BlockSpec((CHUNK, LANES), idx),
                         pl.BlockSpec((2 * SSD_HEADS, CHUNK), idx),
                         pl.BlockSpec((2 * SSD_HEADS, CHUNK), idx)]
    y_shape = jax.ShapeDtypeStruct((t, D_SSD), BF16)
    state = pltpu.VMEM((SSD_HEADS, D_STATE, PAD_HEAD), F32)
    return pl.pallas_call(
        _ssd_kernel,
        grid=(bsz, nc),
        in_specs=specs(fwd) + specs(bwd) + [
            _const_spec((1, D_SSD))],
        out_specs=[pl.BlockSpec((CHUNK, D_SSD), fwd), pl.BlockSpec((CHUNK, D_SSD), bwd)],
        out_shape=[y_shape, y_shape],
        scratch_shapes=[state, state],
        compiler_params=pltpu.CompilerParams(
            dimension_semantics=("parallel", "arbitrary"), vmem_limit_bytes=VMEM_LIMIT),
        name="ssd",
    )(xbc, bt, pot, dtt, pott, xbc, bt, pot, dtt, pott, d_full)


def _mixout_kernel(x_ref, yf_ref, yb_ref, z_ref, u_ref, v_ref, norm_g_ref,
                   sw_ref, sb_ref, wo_ssd_ref, wo_gmlp_ref, post_g_ref, o_ref, yg_ref):
    y = yf_ref[...].astype(F32) + yb_ref[...].astype(F32)
    y_ssd = _rms(y * _silu(z_ref[...].astype(F32)), norm_g_ref[...]).astype(BF16)
    m = _dot(y_ssd, wo_ssd_ref[...])
    for ci in range(TOKEN_TILE // CHUNK):
        rows = slice(ci * CHUNK, (ci + 1) * CHUNK)
        v = v_ref[rows, :]
        mixed = jnp.concatenate(
            [_dot(sw_ref[g], v[:, g * GMLP_GROUP_DIM:(g + 1) * GMLP_GROUP_DIM])
             for g in range(GMLP_GROUPS)], axis=1)
        yg_ref[rows, :] = (u_ref[rows, :].astype(F32) * (mixed + sb_ref[...])).astype(BF16)
    m = m + _dot(yg_ref[...], wo_gmlp_ref[...])
    o_ref[...] = x_ref[...] + _rms(m, post_g_ref[...])


def _mixout(x, yf, yb, z, u, v, norm_g, sw, sb_full, wo_ssd, wo_gmlp, post_g):
    t = x.shape[0]
    row = pl.BlockSpec((TOKEN_TILE, D_MODEL), lambda i: (i, 0))
    return pl.pallas_call(
        _mixout_kernel,
        grid=(t // TOKEN_TILE,),
        in_specs=[row] * 6 + [
            _const_spec((1, D_SSD)), _const_spec((GMLP_GROUPS, CHUNK, CHUNK)),
            _const_spec((CHUNK, D_GMLP)), _const_spec((D_SSD, D_MODEL)),
            _const_spec((D_GMLP, D_MODEL)), _const_spec((1, D_MODEL))],
        out_specs=row,
        out_shape=jax.ShapeDtypeStruct((t, D_MODEL), F32),
        scratch_shapes=[pltpu.VMEM((TOKEN_TILE, D_GMLP), BF16)],
        compiler_params=pltpu.CompilerParams(
            dimension_semantics=("parallel",), vmem_limit_bytes=VMEM_LIMIT),
        name="mixout",
    )(x, yf, yb, z, u, v, norm_g, sw, sb_full, wo_ssd, wo_gmlp, post_g)


def _pad_lanes(row):
    return jnp.pad(row, ((0, 0), (0, LANES - row.shape[1])))


def kernel(x, ff1_pre_g, ff1_w_gate, ff1_w_up, ff1_w_down, ff1_post_g, mix_pre_g, w_in, conv_w, conv_b, dt_bias_f, dt_bias_b, a_log_f, a_log_b, d_skip, ssd_norm_g, gmlp_ln_g, gmlp_ln_b, spatial_w, spatial_b, w_out, mix_post_g, ff2_pre_g, ff2_w_gate, ff2_w_up, ff2_w_down, ff2_post_g):
    bsz, seqlen, _ = x.shape
    depth = w_in.shape[0]
    nc = seqlen // CHUNK
    t = bsz * seqlen
    assert seqlen % TOKEN_TILE == 0 and TOKEN_TILE % CHUNK == 0
    xt = x.reshape(t, D_MODEL)
    row = lambda p: p.reshape(1, -1)
    o_z, o_xbc = D_SSD, D_SSD + CONV_DIM
    o_dt, o_u = o_xbc, o_xbc + 2 * SSD_HEADS
    o_v = o_u + D_GMLP
    for l in range(depth):
        xt = _ffn(xt, row(ff1_pre_g[l]), ff1_w_gate[l].astype(BF16), ff1_w_up[l].astype(BF16),
                  ff1_w_down[l].astype(BF16), row(ff1_post_g[l]))
        w = w_in[l]
        w_dt = jnp.pad(w[:, o_dt:o_u], ((0, 0), (0, LANES - 2 * SSD_HEADS))).astype(BF16)
        dt_bias = _pad_lanes(jnp.concatenate([dt_bias_f[l], dt_bias_b[l]]).reshape(1, -1))
        a_log_row = _pad_lanes(jnp.concatenate([a_log_f[l], a_log_b[l]]).reshape(1, -1))
        conv_w_pad = jnp.pad(conv_w[l], ((0, SUBLANES - CONV_WIDTH), (0, 0)))
        z, xbc, bt, pot, dtt, pott, u, v = _inproj(
            xt, row(mix_pre_g[l]), w[:, :o_z].astype(BF16), w[:, o_z:o_xbc].astype(BF16), w_dt,
            w[:, o_u:o_v].astype(BF16), w[:, o_v:].astype(BF16), dt_bias, a_log_row,
            row(gmlp_ln_g[l]), row(gmlp_ln_b[l]), conv_w_pad, row(conv_b[l]), seqlen)
        d_full = jnp.repeat(d_skip[l], SSD_HEAD_DIM).reshape(1, D_SSD)
        yf, yb = _ssd(xbc, bt, pot, dtt, pott, d_full, bsz, nc)
        sb_full = jnp.repeat(spatial_b[l].T, GMLP_GROUP_DIM, axis=1)
        xt = _mixout(xt, yf, yb, z, u, v, row(ssd_norm_g[l]), spatial_w[l].astype(BF16), sb_full,
                     w_out[l, :D_SSD].astype(BF16), w_out[l, D_SSD:].astype(BF16), row(mix_post_g[l]))
        xt = _ffn(xt, row(ff2_pre_g[l]), ff2_w_gate[l].astype(BF16), ff2_w_up[l].astype(BF16),
                  ff2_w_down[l].astype(BF16), row(ff2_post_g[l]))
    return xt.reshape(bsz, seqlen, D_MODEL)
```
